```python
import math
import jax, jax.numpy as jnp
from jax import lax
import numpy as np

D_MODEL = 2048
BATCH = 16
SEQ = 2048
DEPTH = 4
DEC_BATCH = 16
DEC_SEQ = 64
PAST_LEN = 2048

CHUNK = 64
N_META = 16
N_HEADS = 8
HEAD_DIM = 64
V_DIM = 2 * HEAD_DIM
ATT_W = N_HEADS * V_DIM
CONV_W = D_MODEL - ATT_W
IN_W = 3 * ATT_W + 2 * CONV_W
CONV_K = 31
ROPE_DIM = HEAD_DIM // 4
ROPE_THETA = 500000.0
D_FF = 4 * D_MODEL
Q_BLOCK = 128
EPS = 1e-6
NEG = -1e30
PAD_CHUNK = 2 ** 30
ATTN_SCALE = HEAD_DIM ** -0.5

kernel_name = "hymba_diffattn_conformer_stream_step"


def rms_norm(x, g):
    xf = x.astype(jnp.float32)
    y = xf * lax.rsqrt(jnp.mean(xf * xf, axis=-1, keepdims=True) + EPS)
    return (y * g.astype(jnp.float32)).astype(x.dtype)


def layer_norm(x, g, b):
    xf = x.astype(jnp.float32)
    xc = xf - jnp.mean(xf, axis=-1, keepdims=True)
    y = xc * lax.rsqrt(jnp.mean(xc * xc, axis=-1, keepdims=True) + EPS)
    return (y * g.astype(jnp.float32) + b.astype(jnp.float32)).astype(x.dtype)


def partial_rope(x, pos):
    half = ROPE_DIM // 2
    inv_freq = ROPE_THETA ** (-(jnp.arange(half, dtype=jnp.float32) * 2.0) / ROPE_DIM)
    ang = pos.astype(jnp.float32)[:, None] * inv_freq[None, :]
    cos = jnp.cos(ang)[:, None, None, :]
    sin = jnp.sin(ang)[:, None, None, :]
    xr = x[..., :ROPE_DIM].astype(jnp.float32)
    x1, x2 = xr[..., :half], xr[..., half:]
    rot = jnp.concatenate([x1 * cos - x2 * sin, x2 * cos + x1 * sin], axis=-1).astype(x.dtype)
    return jnp.concatenate([rot, x[..., ROPE_DIM:]], axis=-1)


def mixer_inputs(h, w_in, q_norm_g, k_norm_g, pos):
    b, t, _ = h.shape
    z = jnp.einsum('btd,de->bte', h, w_in)
    q = z[..., :ATT_W].reshape(b, t, N_HEADS, 2, HEAD_DIM)
    k = z[..., ATT_W:2 * ATT_W].reshape(b, t, N_HEADS, 2, HEAD_DIM)
    v = z[..., 2 * ATT_W:3 * ATT_W].reshape(b, t, N_HEADS, V_DIM)
    a, gate = jnp.split(z[..., 3 * ATT_W:], 2, axis=-1)
    g = a * jax.nn.sigmoid(gate)
    q = partial_rope(rms_norm(q, q_norm_g), pos)
    k = partial_rope(rms_norm(k, k_norm_g), pos)
    return q, k, v, g


def lambda_init(layer):
    return 0.8 - 0.6 * math.exp(-0.3 * layer)


def diff_lambda(lq1, lk1, lq2, lk2, lam0):
    f = jnp.float32
    return (jnp.exp(jnp.sum(lq1.astype(f) * lk1.astype(f)))
            - jnp.exp(jnp.sum(lq2.astype(f) * lk2.astype(f))) + lam0)


def diff_attend(q, k, v, mask, lam):
    s = jnp.einsum('bqhcd,bkhcd->bhcqk', q.astype(jnp.float32), k.astype(jnp.float32)) * ATTN_SCALE
    if mask is not None:
        s = jnp.where(mask, s, NEG)
    p = jax.nn.softmax(s, axis=-1)
    a = p[:, :, 0] - lam * p[:, :, 1]
    return jnp.einsum('bhqk,bkhe->bqhe', a, v.astype(jnp.float32))


def chunk_causal_diff_attention(q, k, v, chunk_id, lam):
    b, L = q.shape[0], q.shape[1]
    nb = -(-L // Q_BLOCK)
    lp = nb * Q_BLOCK
    qp = jnp.pad(q, ((0, 0), (0, lp - L), (0, 0), (0, 0), (0, 0)))
    cq = jnp.pad(chunk_id, (0, lp - L), constant_values=PAD_CHUNK)
    qb = jnp.moveaxis(qp.reshape(b, nb, Q_BLOCK, N_HEADS, 2, HEAD_DIM), 1, 0)
    cb = cq.reshape(nb, Q_BLOCK)

    def one_block(args):
        q_blk, c_blk = args
        mask = chunk_id[None, :] <= c_blk[:, None]
        return diff_attend(q_blk, k, v, mask, lam)

    o = lax.map(one_block, (qb, cb))
    o = jnp.moveaxis(o, 0, 1).reshape(b, lp, N_HEADS, V_DIM)
    return o[:, :L]


def diff_head_out(o, g, lam0, dtype):
    o = rms_norm(o, g) * (1.0 - lam0)
    return o.reshape(o.shape[0], o.shape[1], ATT_W).astype(dtype)


def conv_module_tail(gpad, conv_w, conv_b, ln_g, ln_b):
    y = lax.conv_general_dilated(gpad.astype(jnp.float32), conv_w.astype(jnp.float32)[:, None, :],
                                 window_strides=(1,), padding='VALID',
                                 dimension_numbers=('NWC', 'WIO', 'NWC'),
                                 feature_group_count=CONV_W)
    y = layer_norm(y + conv_b.astype(jnp.float32), ln_g, ln_b)
    return jax.nn.silu(y).astype(gpad.dtype)


def sq_relu_mlp(h, w_up, w_down):
    a = jnp.einsum('btd,df->btf', h, w_up)
    return jnp.einsum('btf,fd->btd', jnp.square(jax.nn.relu(a)), w_down)


def finish_layer(x, att, gpad, conv_w, conv_b, ln_g, ln_b, w_out, n2, w_up, w_down):
    c = conv_module_tail(gpad, conv_w, conv_b, ln_g, ln_b)
    mixed = jnp.concatenate([att, c], axis=-1)
    x = x + jnp.einsum('bte,ed->btd', mixed, w_out)
    return x + sq_relu_mlp(rms_norm(x, n2), w_up, w_down)


def setup_inputs(seed: int = 0) -> dict:
    key = jax.random.key(seed)
    ks = jax.random.split(key, 24)
    f = jnp.float32

    def nrm(k, shape, s):
        return jax.random.normal(k, shape, f) * s

    return {
        "x_prompt": nrm(ks[0], (BATCH, SEQ, D_MODEL), 1.0),
        "x_sample": nrm(ks[1], (DEC_BATCH, DEC_SEQ, D_MODEL), 1.0),
        "cache_k": nrm(ks[2], (DEPTH, DEC_BATCH, PAST_LEN, N_HEADS, V_DIM), 1.0),
        "cache_v": nrm(ks[3], (DEPTH, DEC_BATCH, PAST_LEN, N_HEADS, V_DIM), 1.0),
        "state_conv": nrm(ks[4], (DEPTH, DEC_BATCH, CONV_K - 1, CONV_W), 0.5),
        "meta_tokens": nrm(ks[5], (N_META, D_MODEL), 1.0),
        "norm1_g": 1.0 + nrm(ks[6], (DEPTH, D_MODEL), 0.02),
        "w_in": nrm(ks[7], (DEPTH, D_MODEL, IN_W), D_MODEL ** -0.5),
        "q_norm_g": 1.0 + nrm(ks[8], (DEPTH, HEAD_DIM), 0.02),
        "k_norm_g": 1.0 + nrm(ks[9], (DEPTH, HEAD_DIM), 0.02),
        "lam_q1": nrm(ks[10], (DEPTH, HEAD_DIM), 0.1),
        "lam_k1": nrm(ks[11], (DEPTH, HEAD_DIM), 0.1),
        "lam_q2": nrm(ks[12], (DEPTH, HEAD_DIM), 0.1),
        "lam_k2": nrm(ks[13], (DEPTH, HEAD_DIM), 0.1),
        "attn_norm_g": 1.0 + nrm(ks[14], (DEPTH, V_DIM), 0.02),
        "conv_w": nrm(ks[15], (DEPTH, CONV_K, CONV_W), CONV_K ** -0.5),
        "conv_b": nrm(ks[16], (DEPTH, CONV_W), 0.01),
        "conv_ln_g": 1.0 + nrm(ks[17], (DEPTH, CONV_W), 0.02),
        "conv_ln_b": nrm(ks[18], (DEPTH, CONV_W), 0.01),
        "w_out": nrm(ks[19], (DEPTH, D_MODEL, D_MODEL), D_MODEL ** -0.5),
        "norm2_g": 1.0 + nrm(ks[20], (DEPTH, D_MODEL), 0.02),
        "w_up": nrm(ks[21], (DEPTH, D_MODEL, D_FF), D_MODEL ** -0.5),
        "w_down": nrm(ks[22], (DEPTH, D_FF, D_MODEL), D_FF ** -0.5),
    }


def reference(x_prompt, x_sample, cache_k, cache_v, state_conv, meta_tokens, norm1_g, w_in,
              q_norm_g, k_norm_g, lam_q1, lam_k1, lam_q2, lam_k2, attn_norm_g, conv_w, conv_b,
              conv_ln_g, conv_ln_b, w_out, norm2_g, w_up, w_down):
    L = N_META + SEQ
    pos_p = jnp.arange(L, dtype=jnp.int32)
    chunk_p = jnp.concatenate([jnp.zeros((N_META,), jnp.int32),
                               1 + jnp.arange(SEQ, dtype=jnp.int32) // CHUNK])
    pos_s = N_META + PAST_LEN + jnp.arange(DEC_SEQ, dtype=jnp.int32)

    xp = jnp.concatenate([jnp.broadcast_to(meta_tokens.astype(x_prompt.dtype)[None],
                                           (BATCH, N_META, D_MODEL)), x_prompt], axis=1)
    xs = x_sample
    k_p, v_p, c_p, k_s, v_s, c_s = [], [], [], [], [], []

    for l in range(DEPTH):
        lam0 = lambda_init(l)
        lam = diff_lambda(lam_q1[l], lam_k1[l], lam_q2[l], lam_k2[l], lam0)

        q, k, v, g = mixer_inputs(rms_norm(xp, norm1_g[l]), w_in[l], q_norm_g[l], k_norm_g[l], pos_p)
        att = diff_head_out(chunk_causal_diff_attention(q, k, v, chunk_p, lam), attn_norm_g[l], lam0, xp.dtype)
        gpad = jnp.pad(g, ((0, 0), (CONV_K - 1, 0), (0, 0)))
        xp = finish_layer(xp, att, gpad, conv_w[l], conv_b[l], conv_ln_g[l], conv_ln_b[l],
                          w_out[l], norm2_g[l], w_up[l], w_down[l])
        k_p.append(k.reshape(BATCH, L, N_HEADS, V_DIM))
        v_p.append(v)
        c_p.append(gpad[:, -(CONV_K - 1):])
        k_meta = jnp.broadcast_to(k[:1, :N_META], (DEC_BATCH, N_META, N_HEADS, 2, HEAD_DIM))
        v_meta = jnp.broadcast_to(v[:1, :N_META], (DEC_BATCH, N_META, N_HEADS, V_DIM))

        qs, ks_, vs_, gs = mixer_inputs(rms_norm(xs, norm1_g[l]), w_in[l], q_norm_g[l], k_norm_g[l], pos_s)
        k_all = jnp.concatenate([k_meta.astype(ks_.dtype),
                                 cache_k[l].reshape(DEC_BATCH, PAST_LEN, N_HEADS, 2, HEAD_DIM).astype(ks_.dtype),
                                 ks_], axis=1)
        v_all = jnp.concatenate([v_meta.astype(vs_.dtype), cache_v[l].astype(vs_.dtype), vs_], axis=1)
        att_s = diff_head_out(diff_attend(qs, k_all, v_all, None, lam), attn_norm_g[l], lam0, xs.dtype)
        gpad_s = jnp.concatenate([state_conv[l].astype(gs.dtype), gs], axis=1)
        xs = finish_layer(xs, att_s, gpad_s, conv_w[l], conv_b[l], conv_ln_g[l], conv_ln_b[l],
                          w_out[l], norm2_g[l], w_up[l], w_down[l])
        k_s.append(ks_.reshape(DEC_BATCH, DEC_SEQ, N_HEADS, V_DIM))
        v_s.append(vs_)
        c_s.append(gpad_s[:, -(CONV_K - 1):])

    y_prompt = xp[:, N_META:]
    y_sample = xs
    return (y_prompt, y_sample, jnp.stack(k_p), jnp.stack(v_p), jnp.stack(c_p),
            jnp.stack(k_s), jnp.stack(v_s), jnp.stack(c_s))
```

```python
import functools
import math

import jax
import jax.numpy as jnp
from jax import lax
from jax.experimental import pallas as pl
from jax.experimental.pallas import tpu as pltpu

F32 = jnp.float32
BF16 = jnp.bfloat16

D_MODEL = 2048
BATCH = 16
SEQ = 2048
DEPTH = 4
DEC_BATCH = 16
DEC_SEQ = 64
PAST_LEN = 2048
CHUNK = 64
N_META = 16
N_HEADS = 8
HEAD_DIM = 64
V_DIM = 2 * HEAD_DIM
ATT_W = N_HEADS * V_DIM
CONV_W = D_MODEL - ATT_W
IN_W = 3 * ATT_W + 2 * CONV_W
CONV_K = 31
ROPE_DIM = HEAD_DIM // 4
ROPE_THETA = 500000.0
D_FF = 4 * D_MODEL
EPS = 1e-6
NEG = -1e30
ATTN_SCALE = HEAD_DIM ** -0.5

SLOT = DEC_SEQ
HIST = 32
LANES = 128
MXU_W = 256
VMEM_LIMIT = 56 * 1024 * 1024


def _cparams(sem):
    return pltpu.CompilerParams(dimension_semantics=sem, vmem_limit_bytes=VMEM_LIMIT)


def _nt_dot(a, b):
    return lax.dot_general(a, b, (((1,), (1,)), ((), ())), preferred_element_type=F32)


def _dot(a, b):
    return jnp.dot(a, b, preferred_element_type=F32)


def _in_proj_kernel(x_ref, g1_ref, w_ref, wg_ref, p_ref, gq_ref, gk_ref, c_ref, sa_ref, sb_ref,
                    q_ref, k_ref, v_ref, g_ref, xn_ref):
    n = pl.program_id(1)

    @pl.when(n == 0)
    def _():
        x = x_ref[...]
        ms = jnp.mean(x * x, axis=-1, keepdims=True)
        xn_ref[...] = (x * lax.rsqrt(ms + EPS) * g1_ref[...]).astype(BF16)

    z = _dot(xn_ref[...], w_ref[...])

    def qk_norm_rope(gain_ref, out_ref):
        for s in range(ATT_W // MXU_W):
            zz = z[:, s * MXU_W:(s + 1) * MXU_W]
            ms = _dot((zz * zz).astype(BF16), p_ref[...])
            y = zz * lax.rsqrt(ms + EPS)
            for hh in range(MXU_W // LANES):
                lo = s * MXU_W + hh * LANES
                yh = y[:, hh * LANES:(hh + 1) * LANES] * gain_ref[...]
                r = (yh * c_ref[...]
                     + pltpu.roll(yh, LANES - ROPE_DIM // 2, 1) * sa_ref[...]
                     + pltpu.roll(yh, ROPE_DIM // 2, 1) * sb_ref[...])
                out_ref[:, lo:lo + LANES] = r.astype(out_ref.dtype)

    @pl.when(n == 0)
    def _():
        qk_norm_rope(gq_ref, q_ref)

    @pl.when(n == 1)
    def _():
        qk_norm_rope(gk_ref, k_ref)

    @pl.when(n == 2)
    def _():
        v_ref[...] = z

    @pl.when(n == 3)
    def _():
        gate = _dot(xn_ref[...], wg_ref[...])
        g_ref[...] = z * jax.nn.sigmoid(gate)


def _in_proj(x, g1, w_in, p256, gq, gk, c_t, sa_t, sb_t, tm):
    m = x.shape[0]
    nt = c_t.shape[0] // tm
    row = lambda i, n: (i, 0)
    const = lambda i, n: (0, 0)
    tab = lambda i, n: (i % nt, 0)
    return pl.pallas_call(
        _in_proj_kernel,
        grid=(m // tm, 4),
        in_specs=[
            pl.BlockSpec((tm, D_MODEL), row),
            pl.BlockSpec((1, D_MODEL), const),
            pl.BlockSpec((D_MODEL, ATT_W), lambda i, n: (0, n)),
            pl.BlockSpec((D_MODEL, CONV_W), lambda i, n: (0, 4)),
            pl.BlockSpec((MXU_W, MXU_W), const),
            pl.BlockSpec((1, LANES), const),
            pl.BlockSpec((1, LANES), const),
            pl.BlockSpec((tm, LANES), tab),
            pl.BlockSpec((tm, LANES), tab),
            pl.BlockSpec((tm, LANES), tab),
        ],
        out_specs=[pl.BlockSpec((tm, ATT_W), row)] * 3 + [pl.BlockSpec((tm, CONV_W), row)],
        out_shape=[
            jax.ShapeDtypeStruct((m, ATT_W), BF16),
            jax.ShapeDtypeStruct((m, ATT_W), F32),
            jax.ShapeDtypeStruct((m, ATT_W), F32),
            jax.ShapeDtypeStruct((m, CONV_W), F32),
        ],
        scratch_shapes=[pltpu.VMEM((tm, D_MODEL), BF16)],
        compiler_params=_cparams(("arbitrary", "arbitrary")),
        name="in_proj",
    )(x, g1, w_in, w_in, p256, gq, gk, c_t, sa_t, sb_t)


def _lam(lq1_ref, lk1_ref, lq2_ref, lk2_ref, lam0):
    a = jnp.sum(lq1_ref[...] * lk1_ref[...], axis=-1, keepdims=True)
    b = jnp.sum(lq2_ref[...] * lk2_ref[...], axis=-1, keepdims=True)
    return jnp.exp(a) - jnp.exp(b) + lam0


def _split_q(q):
    lane = lax.broadcasted_iota(jnp.int32, q.shape, 1)
    zero = jnp.zeros_like(q)
    return jnp.where(lane < HEAD_DIM, q, zero), jnp.where(lane >= HEAD_DIM, q, zero)


def _pad_meta(ref):
    top = ref[0:N_META, :].astype(BF16)
    return jnp.concatenate([top, jnp.zeros((LANES - N_META, LANES), BF16)], axis=0)


def _meta_scores(qc, kmp):
    s = _nt_dot(qc, kmp)
    lane = lax.broadcasted_iota(jnp.int32, s.shape, 1)
    return jnp.where(lane < N_META, s, NEG)


def _softmax_pv(pieces):
    m = None
    for s, _ in pieces:
        r = jnp.max(s, axis=-1, keepdims=True)
        m = r if m is None else jnp.maximum(m, r)
    l = None
    o = None
    for s, v in pieces:
        e = jnp.exp(s - m)
        ls = jnp.sum(e, axis=-1, keepdims=True)
        os_ = _dot(e.astype(BF16), v)
        l = ls if l is None else l + ls
        o = os_ if o is None else o + os_
    return o / l


def _head_out(o1, o2, lam, gn_ref, lam0):
    o = o1 - lam * o2
    y = o * lax.rsqrt(jnp.mean(o * o, axis=-1, keepdims=True) + EPS) * gn_ref[...]
    return y * (1.0 - lam0)


def _attn_frames_kernel(lq1_ref, lk1_ref, lq2_ref, lk2_ref, q_ref, k_ref, v_ref, km_ref, vm_ref,
                        gn_ref, o_ref, kb_ref, vb_ref, q1_ref, q2_ref, *, lam0, tq, seq):
    lam = _lam(lq1_ref, lk1_ref, lq2_ref, lk2_ref, lam0)
    kb_ref[...] = k_ref[...].astype(BF16)
    vb_ref[...] = v_ref[...].astype(BF16)
    q1, q2 = _split_q(q_ref[...])
    q1_ref[...] = q1
    q2_ref[...] = q2
    kmp = _pad_meta(km_ref)
    vmp = _pad_meta(vm_ref)
    ri = lax.broadcasted_iota(jnp.int32, (tq, tq), 0) // CHUNK
    ci = lax.broadcasted_iota(jnp.int32, (tq, tq), 1) // CHUNK
    diag_ok = ci <= ri

    for i in range(seq // tq):
        lo = i * tq
        outs = []
        for qc_ref in (q1_ref, q2_ref):
            qc = qc_ref[lo:lo + tq, :]
            pieces = [(_meta_scores(qc, kmp), vmp)]
            sd = jnp.where(diag_ok, _nt_dot(qc, kb_ref[lo:lo + tq, :]), NEG)
            pieces.append((sd, vb_ref[lo:lo + tq, :]))
            if i > 0:
                pieces.append((_nt_dot(qc, kb_ref[0:lo, :]), vb_ref[0:lo, :]))
            outs.append(_softmax_pv(pieces))
        o_ref[lo:lo + tq, :] = _head_out(outs[0], outs[1], lam, gn_ref, lam0).astype(o_ref.dtype)


def _attn_frames(lam_p, q, k, v, k_small, v_small, gn, lam0, tq):
    nb = q.shape[0] // SEQ
    lam_spec = pl.BlockSpec((1, HEAD_DIM), lambda b, h: (0, 0))
    blk = pl.BlockSpec((SEQ, LANES), lambda b, h: (b, h))
    meta = pl.BlockSpec((SLOT, LANES), lambda b, h: (DEC_BATCH, h))
    return pl.pallas_call(
        functools.partial(_attn_frames_kernel, lam0=lam0, tq=tq, seq=SEQ),
        grid=(nb, N_HEADS),
        in_specs=[lam_spec] * 4 + [blk, blk, blk, meta, meta, pl.BlockSpec((1, LANES), lambda b, h: (0, 0))],
        out_specs=blk,
        out_shape=jax.ShapeDtypeStruct(q.shape, BF16),
        scratch_shapes=[pltpu.VMEM((SEQ, LANES), BF16)] * 4,
        compiler_params=_cparams(("arbitrary", "arbitrary")),
        name="attn_frames",
    )(*lam_p, q, k, v, k_small, v_small, gn)


def _attn_small_kernel(lq1_ref, lk1_ref, lq2_ref, lk2_ref, q_ref, k_ref, v_ref, km_ref, vm_ref,
                       ck_ref, cv_ref, gn_ref, o_ref, *, lam0):
    s = pl.program_id(0)
    lam = _lam(lq1_ref, lk1_ref, lq2_ref, lk2_ref, lam0)
    q1, q2 = _split_q(q_ref[...])
    kmp = _pad_meta(km_ref)
    vmp = _pad_meta(vm_ref)

    @pl.when(s < DEC_BATCH)
    def _():
        kc = ck_ref[...].astype(BF16)
        vc = cv_ref[...].astype(BF16)
        ks = k_ref[...].astype(BF16)
        vs = v_ref[...].astype(BF16)
        outs = []
        for qc in (q1, q2):
            pieces = [(_meta_scores(qc, kmp), vmp), (_nt_dot(qc, kc), vc), (_nt_dot(qc, ks), vs)]
            outs.append(_softmax_pv(pieces))
        o_ref[...] = _head_out(outs[0], outs[1], lam, gn_ref, lam0).astype(o_ref.dtype)

    @pl.when(s == DEC_BATCH)
    def _():
        outs = [_softmax_pv([(_meta_scores(qc, kmp), vmp)]) for qc in (q1, q2)]
        o_ref[...] = _head_out(outs[0], outs[1], lam, gn_ref, lam0).astype(o_ref.dtype)


def _attn_small(lam_p, q, k, v, cache_k, cache_v, layer, gn, lam0):
    slots = q.shape[0] // SLOT
    lam_spec = pl.BlockSpec((1, HEAD_DIM), lambda s, h: (0, 0))
    blk = pl.BlockSpec((SLOT, LANES), lambda s, h: (s, h))
    meta = pl.BlockSpec((SLOT, LANES), lambda s, h: (DEC_BATCH, h))
    cache = pl.BlockSpec((None, None, PAST_LEN, LANES),
                         lambda s, h: (layer, jnp.minimum(s, DEC_BATCH - 1), 0, h))
    return pl.pallas_call(
        functools.partial(_attn_small_kernel, lam0=lam0),
        grid=(slots, N_HEADS),
        in_specs=[lam_spec] * 4 + [blk, blk, blk, meta, meta, cache, cache,
                                   pl.BlockSpec((1, LANES), lambda s, h: (0, 0))],
        out_specs=blk,
        out_shape=jax.ShapeDtypeStruct(q.shape, BF16),
        compiler_params=_cparams(("arbitrary", "arbitrary")),
        name="attn_small",
    )(*lam_p, q, k, v, k, v, cache_k, cache_v, gn)


CONV_ROWS = 16


def _conv_kernel(hist_ref, g_ref, prev_ref, w_ref, b_ref, lg_ref, lb_ref, c_ref, buf_ref, *, tt):
    j = pl.program_id(1)

    @pl.when(j == 0)
    def _():
        buf_ref[0:HIST, :] = hist_ref[...]

    @pl.when(j > 0)
    def _():
        buf_ref[0:HIST, :] = prev_ref[...]

    buf_ref[HIST:HIST + tt, :] = g_ref[...]
    off = HIST - (CONV_K - 1)

    def step(r, carry):
        r0 = pl.multiple_of(r * CONV_ROWS, CONV_ROWS)
        xw = buf_ref[pl.ds(r0, CONV_ROWS + HIST), :]
        acc = jnp.zeros((CONV_ROWS, CONV_W), F32)
        for r in range(8):
            shifts = [s for s in range(off, off + CONV_K) if s % 8 == r]
            span = max(shifts) - r + CONV_ROWS
            xr = xw[r:r + span, :]
            for s in shifts:
                tap = s - off
                acc = acc + xr[s - r:s - r + CONV_ROWS, :] * w_ref[tap:tap + 1, :]
        y = acc + b_ref[...]
        yc = y - jnp.mean(y, axis=-1, keepdims=True)
        yn = yc * lax.rsqrt(jnp.mean(yc * yc, axis=-1, keepdims=True) + EPS)
        yn = yn * lg_ref[...] + lb_ref[...]
        c_ref[pl.ds(r0, CONV_ROWS), :] = (yn * jax.nn.sigmoid(yn)).astype(c_ref.dtype)
        return carry

    lax.fori_loop(0, tt // CONV_ROWS, step, 0)


def _conv_tail(hist, g, w, b, lg, lb, tt):
    nb, t, _ = g.shape
    nh = hist.shape[0]
    per = tt // HIST
    vec = pl.BlockSpec((1, CONV_W), lambda bi, j: (0, 0))
    return pl.pallas_call(
        functools.partial(_conv_kernel, tt=tt),
        grid=(nb, t // tt),
        in_specs=[
            pl.BlockSpec((None, HIST, CONV_W), lambda bi, j: (bi if nh > 1 else 0, 0, 0)),
            pl.BlockSpec((None, tt, CONV_W), lambda bi, j: (bi, j, 0)),
            pl.BlockSpec((None, HIST, CONV_W), lambda bi, j: (bi, jnp.maximum(j * per - 1, 0), 0)),
            pl.BlockSpec((CONV_K, CONV_W), lambda bi, j: (0, 0)),
            vec, vec, vec,
        ],
        out_specs=pl.BlockSpec((None, tt, CONV_W), lambda bi, j: (bi, j, 0)),
        out_shape=jax.ShapeDtypeStruct(g.shape, BF16),
        scratch_shapes=[pltpu.VMEM((HIST + tt, CONV_W), F32)],
        compiler_params=_cparams(("arbitrary", "arbitrary")),
        name="conv_tail",
    )(hist, g, g, w, b, lg, lb)


def _out_proj_kernel(att_ref, c_ref, x_ref, wa_ref, wc_ref, n2_ref, x1_ref, h2_ref):
    x1 = x_ref[...] + _dot(att_ref[...], wa_ref[...]) + _dot(c_ref[...], wc_ref[...])
    x1_ref[...] = x1
    ms = jnp.mean(x1 * x1, axis=-1, keepdims=True)
    h2_ref[...] = (x1 * lax.rsqrt(ms + EPS) * n2_ref[...]).astype(BF16)


def _out_proj(att, c, x, w_out, n2, tm):
    m = x.shape[0]
    row = lambda i: (i, 0)
    return pl.pallas_call(
        _out_proj_kernel,
        grid=(m // tm,),
        in_specs=[
            pl.BlockSpec((tm, ATT_W), row),
            pl.BlockSpec((tm, CONV_W), row),
            pl.BlockSpec((tm, D_MODEL), row),
            pl.BlockSpec((ATT_W, D_MODEL), lambda i: (0, 0)),
            pl.BlockSpec((CONV_W, D_MODEL), lambda i: (1, 0)),
            pl.BlockSpec((1, D_MODEL), lambda i: (0, 0)),
        ],
        out_specs=[pl.BlockSpec((tm, D_MODEL), row)] * 2,
        out_shape=[jax.ShapeDtypeStruct((m, D_MODEL), F32), jax.ShapeDtypeStruct((m, D_MODEL), BF16)],
        compiler_params=_cparams(("arbitrary",)),
        name="out_proj",
    )(att, c, x, w_out, w_out, n2)


def _mlp_kernel(h_ref, x1_ref, wu_ref, wd_ref, o_ref):
    f = pl.program_id(1)

    @pl.when(f == 0)
    def _():
        o_ref[...] = x1_ref[...]

    a = jnp.maximum(_dot(h_ref[...], wu_ref[...]), 0.0)
    o_ref[...] += _dot((a * a).astype(BF16), wd_ref[...])


def _mlp(h2, x1, w_up, w_down, tm, tf):
    m = x1.shape[0]
    row = lambda i, f: (i, 0)
    return pl.pallas_call(
        _mlp_kernel,
        grid=(m // tm, D_FF // tf),
        in_specs=[
            pl.BlockSpec((tm, D_MODEL), row),
            pl.BlockSpec((tm, D_MODEL), row),
            pl.BlockSpec((D_MODEL, tf), lambda i, f: (0, f)),
            pl.BlockSpec((tf, D_MODEL), lambda i, f: (f, 0)),
        ],
        out_specs=pl.BlockSpec((tm, D_MODEL), row),
        out_shape=jax.ShapeDtypeStruct((m, D_MODEL), F32),
        compiler_params=_cparams(("arbitrary", "arbitrary")),
        name="mlp",
    )(h2, x1, w_up, w_down)


def _rope_tables(pos):
    half = ROPE_DIM // 2
    inv_freq = ROPE_THETA ** (-(jnp.arange(half, dtype=F32) * 2.0) / ROPE_DIM)
    ang = pos.astype(F32)[:, None] * inv_freq[None, :]
    cos, sin = jnp.cos(ang), jnp.sin(ang)
    t = pos.shape[0]
    z8 = jnp.zeros((t, half), F32)
    rest0 = jnp.zeros((t, HEAD_DIM - ROPE_DIM), F32)
    c = jnp.concatenate([cos, cos, 1.0 + rest0], axis=-1)
    sa = jnp.concatenate([-sin, z8, rest0], axis=-1)
    sb = jnp.concatenate([z8, sin, rest0], axis=-1)
    return tuple(jnp.tile(a, (1, LANES // HEAD_DIM)) for a in (c, sa, sb))


def _lambda_init(layer):
    return 0.8 - 0.6 * math.exp(-0.3 * layer)


def _tile_div(n, want):
    t = want
    while n % t:
        t //= 2
    return t


def kernel(x_prompt, x_sample, cache_k, cache_v, state_conv, meta_tokens, norm1_g, w_in, q_norm_g, k_norm_g,
           lam_q1, lam_k1, lam_q2, lam_k2, attn_norm_g, conv_w, conv_b, conv_ln_g, conv_ln_b, w_out, norm2_g,
           w_up, w_down):
    assert DEC_SEQ == SLOT and N_META <= SLOT and SEQ % CHUNK == 0
    slots = DEC_BATCH + 1
    ms = slots * SLOT
    mf = BATCH * SEQ

    xf = x_prompt.reshape(mf, D_MODEL)
    meta_slot = jnp.concatenate([meta_tokens.astype(F32), jnp.zeros((SLOT - N_META, D_MODEL), F32)], axis=0)
    xs = jnp.concatenate([x_sample.reshape(DEC_BATCH * DEC_SEQ, D_MODEL), meta_slot], axis=0)

    pos_f = N_META + jnp.arange(SEQ, dtype=jnp.int32)
    pos_s = jnp.concatenate([
        jnp.tile(N_META + PAST_LEN + jnp.arange(DEC_SEQ, dtype=jnp.int32), DEC_BATCH),
        jnp.arange(N_META, dtype=jnp.int32), jnp.zeros((SLOT - N_META,), jnp.int32)])
    tab_f = _rope_tables(pos_f)
    tab_s = _rope_tables(pos_s)

    grp = jnp.arange(MXU_W) // HEAD_DIM
    p256 = jnp.where(grp[:, None] == grp[None, :], 1.0 / HEAD_DIM, 0.0).astype(BF16)

    w_in_b = w_in.astype(BF16)
    w_out_b = w_out.astype(BF16)
    w_up_b = w_up.astype(BF16)
    w_down_b = w_down.astype(BF16)
    ck = cache_k.reshape(DEPTH, DEC_BATCH, PAST_LEN, ATT_W)
    cv = cache_v.reshape(DEPTH, DEC_BATCH, PAST_LEN, ATT_W)

    tm_f = _tile_div(mf, 512)
    tm_s = ms // 2
    tq = _tile_div(SEQ, 256)
    tt = _tile_div(SEQ, 512)

    k_p, v_p, c_p, k_s, v_s, c_s = [], [], [], [], [], []
    for l in range(DEPTH):
        lam0 = _lambda_init(l)
        lam_p = tuple(a[l].reshape(1, HEAD_DIM) for a in (lam_q1, lam_k1, lam_q2, lam_k2))
        g1 = norm1_g[l].reshape(1, D_MODEL)
        gq = jnp.tile(q_norm_g[l] * ATTN_SCALE, LANES // HEAD_DIM).reshape(1, LANES)
        gk = jnp.tile(k_norm_g[l], LANES // HEAD_DIM).reshape(1, LANES)
        gn = attn_norm_g[l].reshape(1, V_DIM)
        cw, cb = conv_w[l], conv_b[l].reshape(1, CONV_W)
        lg, lb = conv_ln_g[l].reshape(1, CONV_W), conv_ln_b[l].reshape(1, CONV_W)
        n2 = norm2_g[l].reshape(1, D_MODEL)

        qs, ks, vs, gs = _in_proj(xs, g1, w_in_b[l], p256, gq, gk, *tab_s, tm_s)
        qf, kf, vf, gf = _in_proj(xf, g1, w_in_b[l], p256, gq, gk, *tab_f, tm_f)

        att_f = _attn_frames(lam_p, qf, kf, vf, ks, vs, gn, lam0, tq)
        att_s = _attn_small(lam_p, qs, ks, vs, ck, cv, l, gn, lam0)

        g_meta = gs[DEC_BATCH * SLOT:DEC_BATCH * SLOT + N_META]
        hist_f = jnp.concatenate([jnp.zeros((HIST - N_META, CONV_W), F32), g_meta], axis=0)[None]
        hist_s = jnp.concatenate([
            jnp.pad(state_conv[l].astype(F32), ((0, 0), (HIST - (CONV_K - 1), 0), (0, 0))),
            jnp.zeros((1, HIST, CONV_W), F32)], axis=0)
        c_f = _conv_tail(hist_f, gf.reshape(BATCH, SEQ, CONV_W), cw, cb, lg, lb, tt).reshape(mf, CONV_W)
        c_sm = _conv_tail(hist_s, gs.reshape(slots, SLOT, CONV_W), cw, cb, lg, lb, SLOT).reshape(ms, CONV_W)

        x1f, h2f = _out_proj(att_f, c_f, xf, w_out_b[l], n2, tm_f)
        x1s, h2s = _out_proj(att_s, c_sm, xs, w_out_b[l], n2, tm_s)
        xf = _mlp(h2f, x1f, w_up_b[l], w_down_b[l], tm_f, 1024)
        xs = _mlp(h2s, x1s, w_up_b[l], w_down_b[l], tm_s, 1024)

        nd = DEC_BATCH * SLOT
        k_meta = jnp.broadcast_to(ks[nd:nd + N_META][None], (BATCH, N_META, ATT_W))
        v_meta = jnp.broadcast_to(vs[nd:nd + N_META][None], (BATCH, N_META, ATT_W))
        k_p.append(jnp.concatenate([k_meta, kf.reshape(BATCH, SEQ, ATT_W)], axis=1))
        v_p.append(jnp.concatenate([v_meta, vf.reshape(BATCH, SEQ, ATT_W)], axis=1))
        g_all = jnp.concatenate([jnp.broadcast_to(hist_f, (BATCH, HIST, CONV_W)),
                                 gf.reshape(BATCH, SEQ, CONV_W)[:, SEQ - HIST:]], axis=1)
        c_p.append(g_all[:, -(CONV_K - 1):])
        k_s.append(ks[:nd].reshape(DEC_BATCH, DEC_SEQ, ATT_W))
        v_s.append(vs[:nd].reshape(DEC_BATCH, DEC_SEQ, ATT_W))
        gs_all = jnp.concatenate([hist_s[:DEC_BATCH], gs[:nd].reshape(DEC_BATCH, DEC_SEQ, CONV_W)], axis=1)
        c_s.append(gs_all[:, -(CONV_K - 1):])

    lp = N_META + SEQ
    y_prompt = xf.reshape(BATCH, SEQ, D_MODEL)
    y_sample = xs[:DEC_BATCH * SLOT].reshape(DEC_BATCH, DEC_SEQ, D_MODEL)
    return (y_prompt, y_sample,
            jnp.stack(k_p).reshape(DEPTH, BATCH, lp, N_HEADS, V_DIM),
            jnp.stack(v_p).reshape(DEPTH, BATCH, lp, N_HEADS, V_DIM),
            jnp.stack(c_p),
            jnp.stack(k_s).reshape(DEPTH, DEC_BATCH, DEC_SEQ, N_HEADS, V_DIM),
            jnp.stack(v_s).reshape(DEPTH, DEC_BATCH, DEC_SEQ, N_HEADS, V_DIM),
            jnp.stack(c_s))
```

```python
import functools
import math

import jax
import jax.numpy as jnp
from jax import lax
from jax.experimental import pallas as pl
from jax.experimental.pallas import tpu as pltpu

F32 = jnp.float32
BF16 = jnp.bfloat16

D_MODEL = 2048
BATCH = 16
SEQ = 2048
DEPTH = 4
DEC_BATCH = 16
DEC_SEQ = 64
PAST_LEN = 2048
CHUNK = 64
N_META = 16
N_HEADS = 8
HEAD_DIM = 64
V_DIM = 2 * HEAD_DIM
ATT_W = N_HEADS * V_DIM
CONV_W = D_MODEL - ATT_W
IN_W = 3 * ATT_W + 2 * CONV_W
CONV_K = 31
ROPE_DIM = HEAD_DIM // 4
ROPE_THETA = 500000.0
D_FF = 4 * D_MODEL
EPS = 1e-6
NEG = -1e30
ATTN_SCALE = HEAD_DIM ** -0.5

SLOT = DEC_SEQ
HIST = 32
LANES = 128
MXU_W = 256
VMEM_LIMIT = 60 * 1024 * 1024


def _cparams(sem):
    return pltpu.CompilerParams(dimension_semantics=sem, vmem_limit_bytes=VMEM_LIMIT)


def _nt_dot(a, b):
    return lax.dot_general(a, b, (((1,), (1,)), ((), ())), preferred_element_type=F32)


def _dot(a, b):
    return jnp.dot(a, b, preferred_element_type=F32)


CONV_ROWS = 32


def _conv_cols(gbuf_ref, rot_ref, y_ref, cw_ref, cb_ref, tm):
    off = HIST - (CONV_K - 1)
    win = HIST + tm

    def col_body(ci, carry):
        c0 = pl.multiple_of(ci * LANES, LANES)
        xcol = gbuf_ref[:, pl.ds(c0, LANES)]
        for res in range(1, 8):
            rot_ref[res - 1] = pltpu.roll(xcol, win - res, 0)

        def row_body(rb, carry2):
            r0 = pl.multiple_of(rb * CONV_ROWS, CONV_ROWS)
            acc = jnp.zeros((CONV_ROWS, LANES), F32)
            for s in range(off, off + CONV_K):
                res = s % 8
                rows = pl.ds(r0 + (s - res), CONV_ROWS)
                src = gbuf_ref[rows, pl.ds(c0, LANES)] if res == 0 else rot_ref[res - 1, rows, :]
                acc = acc + src * cw_ref[s - off:s - off + 1, pl.ds(c0, LANES)]
            y_ref[pl.ds(r0, CONV_ROWS), pl.ds(c0, LANES)] = acc + cb_ref[:, pl.ds(c0, LANES)]
            return carry2

        lax.fori_loop(0, tm // CONV_ROWS, row_body, 0)
        return carry

    lax.fori_loop(0, CONV_W // LANES, col_body, 0)


def _ln_silu(y_ref, lg_ref, lb_ref, c_ref):
    y = y_ref[...]
    yc = y - jnp.mean(y, axis=-1, keepdims=True)
    yn = yc * lax.rsqrt(jnp.mean(yc * yc, axis=-1, keepdims=True) + EPS)
    yn = yn * lg_ref[...] + lb_ref[...]
    c_ref[...] = (yn * jax.nn.sigmoid(yn)).astype(c_ref.dtype)


def _in_proj_kernel(x_ref, g1_ref, w_ref, p_ref, gq_ref, gk_ref, c_ref, sa_ref, sb_ref, *rest, tm, fuse_conv, tiles_per_seq):
    if fuse_conv:
        (hist_ref, cw_ref, cb_ref, lg_ref, lb_ref,
         q_ref, k_ref, v_ref, kb_ref, vb_ref, co_ref, gt_ref, gbuf_ref, rot_ref, y_ref) = rest
    else:
        q_ref, k_ref, v_ref, kb_ref, vb_ref, g_ref = rest
    x = x_ref[...]
    ms = jnp.mean(x * x, axis=-1, keepdims=True)
    xn = (x * lax.rsqrt(ms + EPS) * g1_ref[...]).astype(BF16)

    def qk_norm_rope(z, gain_ref, out_refs):
        for s in range(ATT_W // MXU_W):
            zz = z[:, s * MXU_W:(s + 1) * MXU_W]
            ms = _dot((zz * zz).astype(BF16), p_ref[...])
            y = zz * lax.rsqrt(ms + EPS)
            for hh in range(MXU_W // LANES):
                lo = s * MXU_W + hh * LANES
                yh = y[:, hh * LANES:(hh + 1) * LANES] * gain_ref[...]
                r = (yh * c_ref[...]
                     + pltpu.roll(yh, LANES - ROPE_DIM // 2, 1) * sa_ref[...]
                     + pltpu.roll(yh, ROPE_DIM // 2, 1) * sb_ref[...])
                for o in out_refs:
                    o[:, lo:lo + LANES] = r.astype(o.dtype)

    a = _dot(xn, w_ref[:, 3 * ATT_W:3 * ATT_W + CONV_W])
    gate = _dot(xn, w_ref[:, 3 * ATT_W + CONV_W:3 * ATT_W + 2 * CONV_W])
    g = a * jax.nn.sigmoid(gate)
    if fuse_conv:
        first = pl.program_id(0) % tiles_per_seq == 0

        @pl.when(first)
        def _():
            gbuf_ref[0:HIST, :] = hist_ref[...]

        @pl.when(jnp.logical_not(first))
        def _():
            gbuf_ref[0:HIST, :] = gbuf_ref[tm:tm + HIST, :]

        gbuf_ref[HIST:HIST + tm, :] = g
        gt_ref[...] = g[tm - HIST:tm, :]
    else:
        g_ref[...] = g
    qk_norm_rope(_dot(xn, w_ref[:, 0:ATT_W]), gq_ref, (q_ref,))
    qk_norm_rope(_dot(xn, w_ref[:, ATT_W:2 * ATT_W]), gk_ref, (k_ref, kb_ref))
    v = _dot(xn, w_ref[:, 2 * ATT_W:3 * ATT_W])
    v_ref[...] = v
    vb_ref[...] = v.astype(BF16)
    if fuse_conv:
        _conv_cols(gbuf_ref, rot_ref, y_ref, cw_ref, cb_ref, tm)
        _ln_silu(y_ref, lg_ref, lb_ref, co_ref)


def _in_proj(x, g1, w_in, layer, p256, gq, gk, c_t, sa_t, sb_t, tm, conv=None, seq=None):
    m = x.shape[0]
    nt = c_t.shape[0] // tm
    row = lambda i: (i, 0)
    const = lambda i: (0, 0)
    tab = lambda i: (i % nt, 0)
    f32_out = jax.ShapeDtypeStruct((m, ATT_W), F32)
    bf_out = jax.ShapeDtypeStruct((m, ATT_W), BF16)
    in_specs = [
        pl.BlockSpec((tm, D_MODEL), row),
        pl.BlockSpec((1, D_MODEL), const),
        pl.BlockSpec((None, D_MODEL, IN_W), lambda i: (layer, 0, 0), pipeline_mode=pl.Buffered(1)),
        pl.BlockSpec((MXU_W, MXU_W), const),
        pl.BlockSpec((1, LANES), const),
        pl.BlockSpec((1, LANES), const),
        pl.BlockSpec((tm, LANES), tab),
        pl.BlockSpec((tm, LANES), tab),
        pl.BlockSpec((tm, LANES), tab),
    ]
    out_specs = [pl.BlockSpec((tm, ATT_W), row)] * 5 + [pl.BlockSpec((tm, CONV_W), row)]
    args = [x, g1, w_in, p256, gq, gk, c_t, sa_t, sb_t]
    if conv is None:
        out_shape = [bf_out, f32_out, f32_out, bf_out, bf_out, jax.ShapeDtypeStruct((m, CONV_W), F32)]
        scratch = []
        tps = 1
    else:
        tps = seq // tm
        vec = pl.BlockSpec((1, CONV_W), const)
        in_specs += [pl.BlockSpec((HIST, CONV_W), const), pl.BlockSpec((CONV_K, CONV_W), const), vec, vec, vec]
        args += list(conv)
        out_specs = out_specs + [pl.BlockSpec((None, HIST, CONV_W), lambda i: (i // tps, 0, 0))]
        out_shape = [bf_out, f32_out, f32_out, bf_out, bf_out, jax.ShapeDtypeStruct((m, CONV_W), BF16),
                     jax.ShapeDtypeStruct((m // seq, HIST, CONV_W), F32)]
        scratch = [pltpu.VMEM((HIST + tm, CONV_W), F32), pltpu.VMEM((7, HIST + tm, LANES), F32),
                   pltpu.VMEM((tm, CONV_W), F32)]
    return pl.pallas_call(
        functools.partial(_in_proj_kernel, tm=tm, fuse_conv=conv is not None, tiles_per_seq=tps),
        grid=(m // tm,),
        in_specs=in_specs,
        out_specs=out_specs,
        out_shape=out_shape,
        scratch_shapes=scratch,
        compiler_params=_cparams(("arbitrary",)),
        name="in_proj",
    )(*args)


def _lam(lq1_ref, lk1_ref, lq2_ref, lk2_ref, lam0):
    a = jnp.sum(lq1_ref[...] * lk1_ref[...], axis=-1, keepdims=True)
    b = jnp.sum(lq2_ref[...] * lk2_ref[...], axis=-1, keepdims=True)
    return jnp.exp(a) - jnp.exp(b) + lam0


def _split_q(q):
    lane = lax.broadcasted_iota(jnp.int32, q.shape, 1)
    zero = jnp.zeros_like(q)
    return jnp.where(lane < HEAD_DIM, q, zero), jnp.where(lane >= HEAD_DIM, q, zero)


def _pad_meta(ref):
    return jnp.concatenate([ref[0:N_META, :], jnp.zeros((LANES - N_META, LANES), BF16)], axis=0)


def _meta_scores(qc, kmp):
    s = _nt_dot(qc, kmp)
    lane = lax.broadcasted_iota(jnp.int32, s.shape, 1)
    return jnp.where(lane < N_META, s, NEG)


def _softmax_pv(pieces):
    m = None
    for s, _ in pieces:
        r = jnp.max(s, axis=-1, keepdims=True)
        m = r if m is None else jnp.maximum(m, r)
    l = None
    o = None
    for s, v in pieces:
        e = jnp.exp2(s - m)
        ls = jnp.sum(e, axis=-1, keepdims=True)
        os_ = _dot(e.astype(BF16), v)
        l = ls if l is None else l + ls
        o = os_ if o is None else o + os_
    return o / l


def _head_out(o1, o2, lam, gn_ref, lam0):
    o = o1 - lam * o2
    y = o * lax.rsqrt(jnp.mean(o * o, axis=-1, keepdims=True) + EPS) * gn_ref[...]
    return y * (1.0 - lam0)


def _attn_frames_kernel(lq1_ref, lk1_ref, lq2_ref, lk2_ref, q_ref, k_ref, v_ref, km_ref, vm_ref,
                        gn_ref, o_ref, vb_ref, q1_ref, q2_ref, s1_ref, s2_ref, *, lam0, tq, seq):
    lam = _lam(lq1_ref, lk1_ref, lq2_ref, lk2_ref, lam0)

    @pl.when((pl.program_id(0) == 0) & (pl.program_id(1) == 0))
    def _():
        vb_ref[:, LANES:2 * LANES] = jnp.ones((LANES + seq, LANES), BF16)

    vb_ref[0:LANES, 0:LANES] = _pad_meta(vm_ref)
    vb_ref[LANES:LANES + seq, 0:LANES] = v_ref[...]
    q1, q2 = _split_q(q_ref[...])
    q1_ref[...] = q1
    q2_ref[...] = q2
    kmp = _pad_meta(km_ref)
    ri = lax.broadcasted_iota(jnp.int32, (tq, tq), 0) // CHUNK
    ci = lax.broadcasted_iota(jnp.int32, (tq, tq), 1) // CHUNK
    diag_ok = ci <= ri

    for i in range(seq // tq):
        lo = i * tq
        outs = []
        for qc_ref, s_ref in ((q1_ref, s1_ref), (q2_ref, s2_ref)):
            qc = qc_ref[lo:lo + tq, :]
            sm = _meta_scores(qc, kmp)
            s_ref[:, 0:LANES] = sm
            mrun = sm
            for j in range(i + 1):
                s = _nt_dot(qc, k_ref[j * tq:(j + 1) * tq, :])
                if j == i:
                    s = jnp.where(diag_ok, s, NEG)
                s_ref[:, LANES + j * tq:LANES + (j + 1) * tq] = s
                for c0 in range(0, tq, LANES):
                    mrun = jnp.maximum(mrun, s[:, c0:c0 + LANES])
            m = jnp.max(mrun, axis=-1, keepdims=True)
            nk = LANES + (i + 1) * tq
            acc = _dot(jnp.exp2(s_ref[:, 0:nk] - m).astype(BF16), vb_ref[0:nk, :])
            outs.append(acc[:, 0:LANES] / acc[:, LANES:2 * LANES])
        o_ref[lo:lo + tq, :] = _head_out(outs[0], outs[1], lam, gn_ref, lam0).astype(o_ref.dtype)


def _attn_frames(lam_p, q, kb, vb, kb_small, vb_small, gn, lam0, tq):
    nb = q.shape[0] // SEQ
    lam_spec = pl.BlockSpec((1, HEAD_DIM), lambda b, h: (0, 0))
    blk = pl.BlockSpec((SEQ, LANES), lambda b, h: (b, h))
    meta = pl.BlockSpec((SLOT, LANES), lambda b, h: (DEC_BATCH, h))
    return pl.pallas_call(
        functools.partial(_attn_frames_kernel, lam0=lam0, tq=tq, seq=SEQ),
        grid=(nb, N_HEADS),
        in_specs=[lam_spec] * 4 + [blk, blk, blk, meta, meta, pl.BlockSpec((1, LANES), lambda b, h: (0, 0))],
        out_specs=blk,
        out_shape=jax.ShapeDtypeStruct(q.shape, BF16),
        scratch_shapes=[pltpu.VMEM((LANES + SEQ, 2 * LANES), BF16),
                        pltpu.VMEM((SEQ, LANES), BF16), pltpu.VMEM((SEQ, LANES), BF16),
                        pltpu.VMEM((tq, LANES + SEQ), F32), pltpu.VMEM((tq, LANES + SEQ), F32)],
        compiler_params=_cparams(("arbitrary", "arbitrary")),
        name="attn_frames",
    )(*lam_p, q, kb, vb, kb_small, vb_small, gn)


def _attn_small_kernel(lq1_ref, lk1_ref, lq2_ref, lk2_ref, q_ref, k_ref, v_ref, km_ref, vm_ref,
                       ck_ref, cv_ref, gn_ref, o_ref, *, lam0):
    s = pl.program_id(0)
    lam = _lam(lq1_ref, lk1_ref, lq2_ref, lk2_ref, lam0)
    q1, q2 = _split_q(q_ref[...])
    kmp = _pad_meta(km_ref)
    vmp = _pad_meta(vm_ref)

    @pl.when(s < DEC_BATCH)
    def _():
        kc = ck_ref[...].astype(BF16)
        vc = cv_ref[...].astype(BF16)
        ks = k_ref[...]
        vs = v_ref[...]
        outs = []
        for qc in (q1, q2):
            pieces = [(_meta_scores(qc, kmp), vmp), (_nt_dot(qc, kc), vc), (_nt_dot(qc, ks), vs)]
            outs.append(_softmax_pv(pieces))
        o_ref[...] = _head_out(outs[0], outs[1], lam, gn_ref, lam0).astype(o_ref.dtype)

    @pl.when(s == DEC_BATCH)
    def _():
        outs = [_softmax_pv([(_meta_scores(qc, kmp), vmp)]) for qc in (q1, q2)]
        o_ref[...] = _head_out(outs[0], outs[1], lam, gn_ref, lam0).astype(o_ref.dtype)


def _attn_small(lam_p, q, k, v, cache_k, cache_v, layer, gn, lam0):
    slots = q.shape[0] // SLOT
    lam_spec = pl.BlockSpec((1, HEAD_DIM), lambda s, h: (0, 0))
    blk = pl.BlockSpec((SLOT, LANES), lambda s, h: (s, h))
    meta = pl.BlockSpec((SLOT, LANES), lambda s, h: (DEC_BATCH, h))
    cache = pl.BlockSpec((None, None, PAST_LEN, LANES),
                         lambda s, h: (layer, jnp.minimum(s, DEC_BATCH - 1), 0, h))
    return pl.pallas_call(
        functools.partial(_attn_small_kernel, lam0=lam0),
        grid=(slots, N_HEADS),
        in_specs=[lam_spec] * 4 + [blk, blk, blk, meta, meta, cache, cache,
                                   pl.BlockSpec((1, LANES), lambda s, h: (0, 0))],
        out_specs=blk,
        out_shape=jax.ShapeDtypeStruct(q.shape, BF16),
        compiler_params=_cparams(("arbitrary", "arbitrary")),
        name="attn_small",
    )(*lam_p, q, k, v, k, v, cache_k, cache_v, gn)


def _conv_kernel(hist_ref, g_ref, w_ref, b_ref, lg_ref, lb_ref, c_ref, gbuf_ref, rot_ref, y_ref, *, tt):
    gbuf_ref[0:HIST, :] = hist_ref[...]
    gbuf_ref[HIST:HIST + tt, :] = g_ref[...]
    _conv_cols(gbuf_ref, rot_ref, y_ref, w_ref, b_ref, tt)
    _ln_silu(y_ref, lg_ref, lb_ref, c_ref)


def _conv_tail(hist, g, w, b, lg, lb):
    nb, tt, _ = g.shape
    vec = pl.BlockSpec((1, CONV_W), lambda bi: (0, 0))
    return pl.pallas_call(
        functools.partial(_conv_kernel, tt=tt),
        grid=(nb,),
        in_specs=[
            pl.BlockSpec((None, HIST, CONV_W), lambda bi: (bi, 0, 0)),
            pl.BlockSpec((None, tt, CONV_W), lambda bi: (bi, 0, 0)),
            pl.BlockSpec((CONV_K, CONV_W), lambda bi: (0, 0)),
            vec, vec, vec,
        ],
        out_specs=pl.BlockSpec((None, tt, CONV_W), lambda bi: (bi, 0, 0)),
        out_shape=jax.ShapeDtypeStruct(g.shape, BF16),
        scratch_shapes=[pltpu.VMEM((HIST + tt, CONV_W), F32), pltpu.VMEM((7, HIST + tt, LANES), F32),
                        pltpu.VMEM((tt, CONV_W), F32)],
        compiler_params=_cparams(("arbitrary",)),
        name="conv_tail",
    )(hist, g, w, b, lg, lb)


def _out_proj_kernel(att_ref, c_ref, x_ref, wa_ref, wc_ref, n2_ref, x1_ref, h2_ref):
    x1 = x_ref[...] + _dot(att_ref[...], wa_ref[...]) + _dot(c_ref[...], wc_ref[...])
    x1_ref[...] = x1
    ms = jnp.mean(x1 * x1, axis=-1, keepdims=True)
    h2_ref[...] = (x1 * lax.rsqrt(ms + EPS) * n2_ref[...]).astype(BF16)


def _out_proj(att, c, x, w_out, layer, n2, tm):
    m = x.shape[0]
    row = lambda i: (i, 0)
    return pl.pallas_call(
        _out_proj_kernel,
        grid=(m // tm,),
        in_specs=[
            pl.BlockSpec((tm, ATT_W), row),
            pl.BlockSpec((tm, CONV_W), row),
            pl.BlockSpec((tm, D_MODEL), row),
            pl.BlockSpec((None, ATT_W, D_MODEL), lambda i: (layer, 0, 0)),
            pl.BlockSpec((None, CONV_W, D_MODEL), lambda i: (layer, 1, 0)),
            pl.BlockSpec((1, D_MODEL), lambda i: (0, 0)),
        ],
        out_specs=[pl.BlockSpec((tm, D_MODEL), row)] * 2,
        out_shape=[jax.ShapeDtypeStruct((m, D_MODEL), F32), jax.ShapeDtypeStruct((m, D_MODEL), BF16)],
        compiler_params=_cparams(("arbitrary",)),
        name="out_proj",
    )(att, c, x, w_out, w_out, n2)


def _mlp_kernel(h_ref, x1_ref, wu_ref, wd_ref, o_ref):
    f = pl.program_id(1)

    @pl.when(f == 0)
    def _():
        o_ref[...] = x1_ref[...]

    a = jnp.maximum(_dot(h_ref[...], wu_ref[...]), 0.0)
    o_ref[...] += _dot((a * a).astype(BF16), wd_ref[...])


def _mlp(h2, x1, w_up, w_down, layer, tm, tf):
    m = x1.shape[0]
    row = lambda i, f: (i, 0)
    return pl.pallas_call(
        _mlp_kernel,
        grid=(m // tm, D_FF // tf),
        in_specs=[
            pl.BlockSpec((tm, D_MODEL), row),
            pl.BlockSpec((tm, D_MODEL), row),
            pl.BlockSpec((None, D_MODEL, tf), lambda i, f: (layer, 0, f)),
            pl.BlockSpec((None, tf, D_MODEL), lambda i, f: (layer, f, 0)),
        ],
        out_specs=pl.BlockSpec((tm, D_MODEL), row),
        out_shape=jax.ShapeDtypeStruct((m, D_MODEL), F32),
        compiler_params=_cparams(("arbitrary", "arbitrary")),
        name="mlp",
    )(h2, x1, w_up, w_down)


def _rope_tables(pos):
    half = ROPE_DIM // 2
    inv_freq = ROPE_THETA ** (-(jnp.arange(half, dtype=F32) * 2.0) / ROPE_DIM)
    ang = pos.astype(F32)[:, None] * inv_freq[None, :]
    cos, sin = jnp.cos(ang), jnp.sin(ang)
    t = pos.shape[0]
    z8 = jnp.zeros((t, half), F32)
    rest0 = jnp.zeros((t, HEAD_DIM - ROPE_DIM), F32)
    c = jnp.concatenate([cos, cos, 1.0 + rest0], axis=-1)
    sa = jnp.concatenate([-sin, z8, rest0], axis=-1)
    sb = jnp.concatenate([z8, sin, rest0], axis=-1)
    return tuple(jnp.tile(a, (1, LANES // HEAD_DIM)) for a in (c, sa, sb))


def _lambda_init(layer):
    return 0.8 - 0.6 * math.exp(-0.3 * layer)


def _tile_div(n, want):
    t = want
    while n % t:
        t //= 2
    return t


def kernel(x_prompt, x_sample, cache_k, cache_v, state_conv, meta_tokens, norm1_g, w_in, q_norm_g, k_norm_g,
           lam_q1, lam_k1, lam_q2, lam_k2, attn_norm_g, conv_w, conv_b, conv_ln_g, conv_ln_b, w_out, norm2_g,
           w_up, w_down):
    assert DEC_SEQ == SLOT and N_META <= SLOT and SEQ % CHUNK == 0 and N_META <= HIST
    slots = DEC_BATCH + 1
    ms = slots * SLOT
    mf = BATCH * SEQ
    nd = DEC_BATCH * SLOT

    xf = x_prompt.reshape(mf, D_MODEL)
    meta_slot = jnp.concatenate([meta_tokens.astype(F32), jnp.zeros((SLOT - N_META, D_MODEL), F32)], axis=0)
    xs = jnp.concatenate([x_sample.reshape(nd, D_MODEL), meta_slot], axis=0)

    pos_f = N_META + jnp.arange(SEQ, dtype=jnp.int32)
    pos_s = jnp.concatenate([
        jnp.tile(N_META + PAST_LEN + jnp.arange(DEC_SEQ, dtype=jnp.int32), DEC_BATCH),
        jnp.arange(N_META, dtype=jnp.int32), jnp.zeros((SLOT - N_META,), jnp.int32)])
    tab_f = _rope_tables(pos_f)
    tab_s = _rope_tables(pos_s)

    grp = jnp.arange(MXU_W) // HEAD_DIM
    p256 = jnp.where(grp[:, None] == grp[None, :], 1.0 / HEAD_DIM, 0.0).astype(BF16)

    w_in_b = w_in.astype(BF16)
    w_out_b = w_out.astype(BF16)
    w_up_b = w_up.astype(BF16)
    w_down_b = w_down.astype(BF16)
    ck = cache_k.reshape(DEPTH, DEC_BATCH, PAST_LEN, ATT_W)
    cv = cache_v.reshape(DEPTH, DEC_BATCH, PAST_LEN, ATT_W)

    tm_f = _tile_div(SEQ, 512)
    tm_s = ms // 2
    tq = _tile_div(SEQ, 256)

    k_p, v_p, c_p, k_s, v_s, c_s = [], [], [], [], [], []
    for l in range(DEPTH):
        lam0 = _lambda_init(l)
        lam_p = tuple(a[l].reshape(1, HEAD_DIM) for a in (lam_q1, lam_k1, lam_q2, lam_k2))
        g1 = norm1_g[l].reshape(1, D_MODEL)
        gq = jnp.tile(q_norm_g[l] * (ATTN_SCALE * math.log2(math.e)), LANES // HEAD_DIM).reshape(1, LANES)
        gk = jnp.tile(k_norm_g[l], LANES // HEAD_DIM).reshape(1, LANES)
        gn = attn_norm_g[l].reshape(1, V_DIM)
        cw, cb = conv_w[l], conv_b[l].reshape(1, CONV_W)
        lg, lb = conv_ln_g[l].reshape(1, CONV_W), conv_ln_b[l].reshape(1, CONV_W)
        n2 = norm2_g[l].reshape(1, D_MODEL)

        qs, ks, vs, kbs, vbs, gs = _in_proj(xs, g1, w_in_b, l, p256, gq, gk, *tab_s, tm_s)
        g_meta = gs[nd:nd + N_META]
        hist_f = jnp.concatenate([jnp.zeros((HIST - N_META, CONV_W), F32), g_meta], axis=0)
        qf, kf, vf, kbf, vbf, c_f, g_tail = _in_proj(xf, g1, w_in_b, l, p256, gq, gk, *tab_f, tm_f,
                                                     conv=(hist_f, cw, cb, lg, lb), seq=SEQ)

        att_f = _attn_frames(lam_p, qf, kbf, vbf, kbs, vbs, gn, lam0, tq)
        att_s = _attn_small(lam_p, qs, kbs, vbs, ck, cv, l, gn, lam0)

        hist_s = jnp.concatenate([
            jnp.pad(state_conv[l].astype(F32), ((0, 0), (HIST - (CONV_K - 1), 0), (0, 0))),
            jnp.zeros((1, HIST, CONV_W), F32)], axis=0)
        c_sm = _conv_tail(hist_s, gs.reshape(slots, SLOT, CONV_W), cw, cb, lg, lb).reshape(ms, CONV_W)

        x1f, h2f = _out_proj(att_f, c_f, xf, w_out_b, l, n2, tm_f)
        x1s, h2s = _out_proj(att_s, c_sm, xs, w_out_b, l, n2, tm_s)
        xf = _mlp(h2f, x1f, w_up_b, w_down_b, l, tm_f, 1024)
        xs = _mlp(h2s, x1s, w_up_b, w_down_b, l, tm_s, 1024)

        k_meta = jnp.broadcast_to(ks[nd:nd + N_META][None], (BATCH, N_META, ATT_W))
        v_meta = jnp.broadcast_to(vs[nd:nd + N_META][None], (BATCH, N_META, ATT_W))
        k_p.append(jnp.concatenate([k_meta, kf.reshape(BATCH, SEQ, ATT_W)], axis=1))
        v_p.append(jnp.concatenate([v_meta, vf.reshape(BATCH, SEQ, ATT_W)], axis=1))
        g_all = jnp.concatenate([jnp.broadcast_to(hist_f[None], (BATCH, HIST, CONV_W)), g_tail], axis=1)
        c_p.append(g_all[:, -(CONV_K - 1):])
        k_s.append(ks[:nd].reshape(DEC_BATCH, DEC_SEQ, ATT_W))
        v_s.append(vs[:nd].reshape(DEC_BATCH, DEC_SEQ, ATT_W))
        gs_all = jnp.concatenate([hist_s[:DEC_BATCH], gs[:nd].reshape(DEC_BATCH, DEC_SEQ, CONV_W)], axis=1)
        c_s.append(gs_all[:, -(CONV_K - 1):])

    lp = N_META + SEQ
    y_prompt = xf.reshape(BATCH, SEQ, D_MODEL)
    y_sample = xs[:nd].reshape(DEC_BATCH, DEC_SEQ, D_MODEL)
    return (y_prompt, y_sample,
            jnp.stack(k_p).reshape(DEPTH, BATCH, lp, N_HEADS, V_DIM),
            jnp.stack(v_p).reshape(DEPTH, BATCH, lp, N_HEADS, V_DIM),
            jnp.stack(c_p),
            jnp.stack(k_s).reshape(DEPTH, DEC_BATCH, DEC_SEQ, N_HEADS, V_DIM),
            jnp.stack(v_s).reshape(DEPTH, DEC_BATCH, DEC_SEQ, N_HEADS, V_DIM),
            jnp.stack(c_s))
```

```python
import functools
import math

import jax
import jax.numpy as jnp
from jax import lax
from jax.experimental import pallas as pl
from jax.experimental.pallas import tpu as pltpu

F32 = jnp.float32
BF16 = jnp.bfloat16

D_MODEL = 2048
BATCH = 16
SEQ = 2048
DEPTH = 4
DEC_BATCH = 16
DEC_SEQ = 64
PAST_LEN = 2048
CHUNK = 64
N_META = 16
N_HEADS = 8
HEAD_DIM = 64
V_DIM = 2 * HEAD_DIM
ATT_W = N_HEADS * V_DIM
CONV_W = D_MODEL - ATT_W
IN_W = 3 * ATT_W + 2 * CONV_W
CONV_K = 31
ROPE_DIM = HEAD_DIM // 4
ROPE_THETA = 500000.0
D_FF = 4 * D_MODEL
EPS = 1e-6
NEG = -1e30
ATTN_SCALE = HEAD_DIM ** -0.5

SLOT = DEC_SEQ
HIST = 32
LANES = 128
MXU_W = 256
VMEM_LIMIT = 60 * 1024 * 1024


def _cparams(sem):
    return pltpu.CompilerParams(dimension_semantics=sem, vmem_limit_bytes=VMEM_LIMIT)


def _nt_dot(a, b):
    return lax.dot_general(a, b, (((1,), (1,)), ((), ())), preferred_element_type=F32)


def _dot(a, b):
    return jnp.dot(a, b, preferred_element_type=F32)


CONV_ROWS = 32


def _conv_cols(gbuf_ref, rot_ref, y_ref, cw_ref, cb_ref, tm):
    off = HIST - (CONV_K - 1)
    win = HIST + tm

    def col_body(ci, carry):
        c0 = pl.multiple_of(ci * LANES, LANES)
        xcol = gbuf_ref[:, pl.ds(c0, LANES)]
        for res in range(1, 8):
            rot_ref[res - 1] = pltpu.roll(xcol, win - res, 0)

        def row_body(rb, carry2):
            r0 = pl.multiple_of(rb * CONV_ROWS, CONV_ROWS)
            acc = jnp.zeros((CONV_ROWS, LANES), F32)
            for s in range(off, off + CONV_K):
                res = s % 8
                rows = pl.ds(r0 + (s - res), CONV_ROWS)
                src = gbuf_ref[rows, pl.ds(c0, LANES)] if res == 0 else rot_ref[res - 1, rows, :]
                acc = acc + src * cw_ref[s - off:s - off + 1, pl.ds(c0, LANES)]
            y_ref[pl.ds(r0, CONV_ROWS), pl.ds(c0, LANES)] = acc + cb_ref[:, pl.ds(c0, LANES)]
            return carry2

        lax.fori_loop(0, tm // CONV_ROWS, row_body, 0)
        return carry

    lax.fori_loop(0, CONV_W // LANES, col_body, 0)


def _ln_silu(y_ref, lg_ref, lb_ref, c_ref):
    y = y_ref[...]
    yc = y - jnp.mean(y, axis=-1, keepdims=True)
    yn = yc * lax.rsqrt(jnp.mean(yc * yc, axis=-1, keepdims=True) + EPS)
    yn = yn * lg_ref[...] + lb_ref[...]
    c_ref[...] = (yn * jax.nn.sigmoid(yn)).astype(c_ref.dtype)


def _in_proj_kernel(x_ref, g1_ref, w_ref, p_ref, gq_ref, gk_ref, c_ref, sa_ref, sb_ref, *rest, tm, fuse_conv, tiles_per_seq):
    if fuse_conv:
        (hist_ref, cw_ref, cb_ref, lg_ref, lb_ref,
         q_ref, k_ref, v_ref, kb_ref, vb_ref, co_ref, gt_ref, gbuf_ref, rot_ref, y_ref) = rest
    else:
        q_ref, k_ref, v_ref, kb_ref, vb_ref, g_ref = rest
    x = x_ref[...]
    ms = jnp.mean(x * x, axis=-1, keepdims=True)
    xn = (x * lax.rsqrt(ms + EPS) * g1_ref[...]).astype(BF16)

    def qk_norm_rope(z, gain_ref, out_refs):
        for s in range(ATT_W // MXU_W):
            zz = z[:, s * MXU_W:(s + 1) * MXU_W]
            ms = _dot((zz * zz).astype(BF16), p_ref[...])
            y = zz * lax.rsqrt(ms + EPS)
            for hh in range(MXU_W // LANES):
                lo = s * MXU_W + hh * LANES
                yh = y[:, hh * LANES:(hh + 1) * LANES] * gain_ref[...]
                r = (yh * c_ref[...]
                     + pltpu.roll(yh, LANES - ROPE_DIM // 2, 1) * sa_ref[...]
                     + pltpu.roll(yh, ROPE_DIM // 2, 1) * sb_ref[...])
                for o in out_refs:
                    o[:, lo:lo + LANES] = r.astype(o.dtype)

    a = _dot(xn, w_ref[:, 3 * ATT_W:3 * ATT_W + CONV_W])
    gate = _dot(xn, w_ref[:, 3 * ATT_W + CONV_W:3 * ATT_W + 2 * CONV_W])
    g = a * jax.nn.sigmoid(gate)
    if fuse_conv:
        first = pl.program_id(0) % tiles_per_seq == 0

        @pl.when(first)
        def _():
            gbuf_ref[0:HIST, :] = hist_ref[...]

        @pl.when(jnp.logical_not(first))
        def _():
            gbuf_ref[0:HIST, :] = gbuf_ref[tm:tm + HIST, :]

        gbuf_ref[HIST:HIST + tm, :] = g
        gt_ref[...] = g[tm - HIST:tm, :]
    else:
        g_ref[...] = g
    qk_norm_rope(_dot(xn, w_ref[:, 0:ATT_W]), gq_ref, (q_ref,))
    qk_norm_rope(_dot(xn, w_ref[:, ATT_W:2 * ATT_W]), gk_ref, (k_ref, kb_ref))
    v = _dot(xn, w_ref[:, 2 * ATT_W:3 * ATT_W])
    v_ref[...] = v
    vb_ref[...] = v.astype(BF16)
    if fuse_conv:
        _conv_cols(gbuf_ref, rot_ref, y_ref, cw_ref, cb_ref, tm)
        _ln_silu(y_ref, lg_ref, lb_ref, co_ref)


def _in_proj(x, g1, w_in, layer, p256, gq, gk, c_t, sa_t, sb_t, tm, conv=None, seq=None):
    m = x.shape[0]
    nt = c_t.shape[0] // tm
    row = lambda i: (i, 0)
    const = lambda i: (0, 0)
    tab = lambda i: (i % nt, 0)
    f32_out = jax.ShapeDtypeStruct((m, ATT_W), F32)
    bf_out = jax.ShapeDtypeStruct((m, ATT_W), BF16)
    in_specs = [
        pl.BlockSpec((tm, D_MODEL), row),
        pl.BlockSpec((1, D_MODEL), const),
        pl.BlockSpec((None, D_MODEL, IN_W), lambda i: (layer, 0, 0), pipeline_mode=pl.Buffered(1)),
        pl.BlockSpec((MXU_W, MXU_W), const),
        pl.BlockSpec((1, LANES), const),
        pl.BlockSpec((1, LANES), const),
        pl.BlockSpec((tm, LANES), tab),
        pl.BlockSpec((tm, LANES), tab),
        pl.BlockSpec((tm, LANES), tab),
    ]
    out_specs = [pl.BlockSpec((tm, ATT_W), row)] * 5 + [pl.BlockSpec((tm, CONV_W), row)]
    args = [x, g1, w_in, p256, gq, gk, c_t, sa_t, sb_t]
    if conv is None:
        out_shape = [bf_out, f32_out, f32_out, bf_out, bf_out, jax.ShapeDtypeStruct((m, CONV_W), F32)]
        scratch = []
        tps = 1
    else:
        tps = seq // tm
        vec = pl.BlockSpec((1, CONV_W), const)
        in_specs += [pl.BlockSpec((HIST, CONV_W), const), pl.BlockSpec((CONV_K, CONV_W), const), vec, vec, vec]
        args += list(conv)
        out_specs = out_specs + [pl.BlockSpec((None, HIST, CONV_W), lambda i: (i // tps, 0, 0))]
        out_shape = [bf_out, f32_out, f32_out, bf_out, bf_out, jax.ShapeDtypeStruct((m, CONV_W), BF16),
                     jax.ShapeDtypeStruct((m // seq, HIST, CONV_W), F32)]
        scratch = [pltpu.VMEM((HIST + tm, CONV_W), F32), pltpu.VMEM((7, HIST + tm, LANES), F32),
                   pltpu.VMEM((tm, CONV_W), F32)]
    return pl.pallas_call(
        functools.partial(_in_proj_kernel, tm=tm, fuse_conv=conv is not None, tiles_per_seq=tps),
        grid=(m // tm,),
        in_specs=in_specs,
        out_specs=out_specs,
        out_shape=out_shape,
        scratch_shapes=scratch,
        compiler_params=_cparams(("arbitrary",)),
        name="in_proj",
    )(*args)


def _lam(lq1_ref, lk1_ref, lq2_ref, lk2_ref, lam0):
    a = jnp.sum(lq1_ref[...] * lk1_ref[...], axis=-1, keepdims=True)
    b = jnp.sum(lq2_ref[...] * lk2_ref[...], axis=-1, keepdims=True)
    return jnp.exp(a) - jnp.exp(b) + lam0


def _split_q(q):
    lane = lax.broadcasted_iota(jnp.int32, q.shape, 1)
    zero = jnp.zeros_like(q)
    return jnp.where(lane < HEAD_DIM, q, zero), jnp.where(lane >= HEAD_DIM, q, zero)


def _pad_meta(ref):
    return jnp.concatenate([ref[0:N_META, :], jnp.zeros((LANES - N_META, LANES), BF16)], axis=0)


def _meta_scores(qc, kmp):
    s = _nt_dot(qc, kmp)
    lane = lax.broadcasted_iota(jnp.int32, s.shape, 1)
    return jnp.where(lane < N_META, s, NEG)


def _softmax_pv(pieces):
    m = None
    for s, _ in pieces:
        r = jnp.max(s, axis=-1, keepdims=True)
        m = r if m is None else jnp.maximum(m, r)
    l = None
    o = None
    for s, v in pieces:
        e = jnp.exp2(s - m)
        ls = jnp.sum(e, axis=-1, keepdims=True)
        os_ = _dot(e.astype(BF16), v)
        l = ls if l is None else l + ls
        o = os_ if o is None else o + os_
    return o / l


def _head_out(o1, o2, lam, gn_ref, lam0):
    o = o1 - lam * o2
    y = o * lax.rsqrt(jnp.mean(o * o, axis=-1, keepdims=True) + EPS) * gn_ref[...]
    return y * (1.0 - lam0)


def _attn_frames_kernel(lq1_ref, lk1_ref, lq2_ref, lk2_ref, q_ref, k_ref, v_ref, km_ref, vm_ref,
                        gn_ref, o_ref, vb_ref, q1_ref, q2_ref, s1_ref, s2_ref, *, lam0, tq, seq):
    lam = _lam(lq1_ref, lk1_ref, lq2_ref, lk2_ref, lam0)

    @pl.when((pl.program_id(0) == 0) & (pl.program_id(1) == 0))
    def _():
        vb_ref[:, LANES:2 * LANES] = jnp.ones((LANES + seq, LANES), BF16)

    vb_ref[0:LANES, 0:LANES] = _pad_meta(vm_ref)
    vb_ref[LANES:LANES + seq, 0:LANES] = v_ref[...]
    q1, q2 = _split_q(q_ref[...])
    q1_ref[...] = q1
    q2_ref[...] = q2
    kmp = _pad_meta(km_ref)
    ri = lax.broadcasted_iota(jnp.int32, (tq, tq), 0) // CHUNK
    ci = lax.broadcasted_iota(jnp.int32, (tq, tq), 1) // CHUNK
    diag_ok = ci <= ri

    for i in range(seq // tq):
        lo = i * tq
        outs = []
        for qc_ref, s_ref in ((q1_ref, s1_ref), (q2_ref, s2_ref)):
            qc = qc_ref[lo:lo + tq, :]
            sm = _meta_scores(qc, kmp)
            s_ref[:, 0:LANES] = sm
            mrun = sm
            for j in range(i + 1):
                s = _nt_dot(qc, k_ref[j * tq:(j + 1) * tq, :])
                if j == i:
                    s = jnp.where(diag_ok, s, NEG)
                s_ref[:, LANES + j * tq:LANES + (j + 1) * tq] = s
                for c0 in range(0, tq, LANES):
                    mrun = jnp.maximum(mrun, s[:, c0:c0 + LANES])
            m = jnp.max(mrun, axis=-1, keepdims=True)
            nk = LANES + (i + 1) * tq
            acc = _dot(jnp.exp2(s_ref[:, 0:nk] - m).astype(BF16), vb_ref[0:nk, :])
            outs.append(acc[:, 0:LANES] / acc[:, LANES:2 * LANES])
        o_ref[lo:lo + tq, :] = _head_out(outs[0], outs[1], lam, gn_ref, lam0).astype(o_ref.dtype)


def _attn_frames(lam_p, q, kb, vb, kb_small, vb_small, gn, lam0, tq):
    nb = q.shape[0] // SEQ
    lam_spec = pl.BlockSpec((1, HEAD_DIM), lambda b, h: (0, 0))
    blk = pl.BlockSpec((SEQ, LANES), lambda b, h: (b, h))
    meta = pl.BlockSpec((SLOT, LANES), lambda b, h: (DEC_BATCH, h))
    return pl.pallas_call(
        functools.partial(_attn_frames_kernel, lam0=lam0, tq=tq, seq=SEQ),
        grid=(nb, N_HEADS),
        in_specs=[lam_spec] * 4 + [blk, blk, blk, meta, meta, pl.BlockSpec((1, LANES), lambda b, h: (0, 0))],
        out_specs=blk,
        out_shape=jax.ShapeDtypeStruct(q.shape, BF16),
        scratch_shapes=[pltpu.VMEM((LANES + SEQ, 2 * LANES), BF16),
                        pltpu.VMEM((SEQ, LANES), BF16), pltpu.VMEM((SEQ, LANES), BF16),
                        pltpu.VMEM((tq, LANES + SEQ), F32), pltpu.VMEM((tq, LANES + SEQ), F32)],
        compiler_params=_cparams(("arbitrary", "arbitrary")),
        name="attn_frames",
    )(*lam_p, q, kb, vb, kb_small, vb_small, gn)


def _attn_small_kernel(lq1_ref, lk1_ref, lq2_ref, lk2_ref, q_ref, k_ref, v_ref, km_ref, vm_ref,
                       ck_ref, cv_ref, gn_ref, o_ref, *, lam0):
    s = pl.program_id(0)
    lam = _lam(lq1_ref, lk1_ref, lq2_ref, lk2_ref, lam0)

    def head(h, with_cache):
        cols = slice(h * LANES, (h + 1) * LANES)
        q1, q2 = _split_q(q_ref[:, cols])
        qq = jnp.concatenate([q1, q2], axis=0)
        kmp = jnp.concatenate([km_ref[0:N_META, cols], jnp.zeros((LANES - N_META, LANES), BF16)], axis=0)
        vmp = jnp.concatenate([vm_ref[0:N_META, cols], jnp.zeros((LANES - N_META, LANES), BF16)], axis=0)
        pieces = [(_meta_scores(qq, kmp), vmp)]
        if with_cache:
            rows = pl.ds(h, PAST_LEN, stride=N_HEADS)
            pieces.append((_nt_dot(qq, ck_ref[rows, :].astype(BF16)), cv_ref[rows, :].astype(BF16)))
            pieces.append((_nt_dot(qq, k_ref[:, cols]), v_ref[:, cols]))
        o = _softmax_pv(pieces)
        o_ref[:, cols] = _head_out(o[0:SLOT], o[SLOT:2 * SLOT], lam, gn_ref, lam0).astype(o_ref.dtype)

    @pl.when(s < DEC_BATCH)
    def _():
        for h in range(N_HEADS):
            head(h, True)

    @pl.when(s == DEC_BATCH)
    def _():
        for h in range(N_HEADS):
            head(h, False)


def _attn_small(lam_p, q, k, v, cache_k, cache_v, layer, gn, lam0):
    slots = q.shape[0] // SLOT
    lam_spec = pl.BlockSpec((1, HEAD_DIM), lambda s: (0, 0))
    blk = pl.BlockSpec((SLOT, ATT_W), lambda s: (s, 0))
    meta = pl.BlockSpec((SLOT, ATT_W), lambda s: (DEC_BATCH, 0))
    cache = pl.BlockSpec((None, None, PAST_LEN * N_HEADS, V_DIM),
                         lambda s: (layer, jnp.minimum(s, DEC_BATCH - 1), 0, 0))
    return pl.pallas_call(
        functools.partial(_attn_small_kernel, lam0=lam0),
        grid=(slots,),
        in_specs=[lam_spec] * 4 + [blk, blk, blk, meta, meta, cache, cache,
                                   pl.BlockSpec((1, V_DIM), lambda s: (0, 0))],
        out_specs=blk,
        out_shape=jax.ShapeDtypeStruct(q.shape, BF16),
        compiler_params=_cparams(("arbitrary",)),
        name="attn_small",
    )(*lam_p, q, k, v, k, v, cache_k, cache_v, gn)


def _conv_kernel(hist_ref, g_ref, w_ref, b_ref, lg_ref, lb_ref, c_ref, gbuf_ref, rot_ref, y_ref, *, tt):
    gbuf_ref[0:HIST, :] = hist_ref[...]
    gbuf_ref[HIST:HIST + tt, :] = g_ref[...]
    _conv_cols(gbuf_ref, rot_ref, y_ref, w_ref, b_ref, tt)
    _ln_silu(y_ref, lg_ref, lb_ref, c_ref)


def _conv_tail(hist, g, w, b, lg, lb):
    nb, tt, _ = g.shape
    vec = pl.BlockSpec((1, CONV_W), lambda bi: (0, 0))
    return pl.pallas_call(
        functools.partial(_conv_kernel, tt=tt),
        grid=(nb,),
        in_specs=[
            pl.BlockSpec((None, HIST, CONV_W), lambda bi: (bi, 0, 0)),
            pl.BlockSpec((None, tt, CONV_W), lambda bi: (bi, 0, 0)),
            pl.BlockSpec((CONV_K, CONV_W), lambda bi: (0, 0)),
            vec, vec, vec,
        ],
        out_specs=pl.BlockSpec((None, tt, CONV_W), lambda bi: (bi, 0, 0)),
        out_shape=jax.ShapeDtypeStruct(g.shape, BF16),
        scratch_shapes=[pltpu.VMEM((HIST + tt, CONV_W), F32), pltpu.VMEM((7, HIST + tt, LANES), F32),
                        pltpu.VMEM((tt, CONV_W), F32)],
        compiler_params=_cparams(("arbitrary",)),
        name="conv_tail",
    )(hist, g, w, b, lg, lb)


def _out_mlp_kernel(att_ref, c_ref, x_ref, wa_ref, wc_ref, n2_ref, wu_ref, wd_ref, o_ref, h2_ref):
    f = pl.program_id(1)

    @pl.when(f == 0)
    def _():
        x1 = x_ref[...] + _dot(att_ref[...], wa_ref[...]) + _dot(c_ref[...], wc_ref[...])
        o_ref[...] = x1
        ms = jnp.mean(x1 * x1, axis=-1, keepdims=True)
        h2_ref[...] = (x1 * lax.rsqrt(ms + EPS) * n2_ref[...]).astype(BF16)

    a = jnp.maximum(_dot(h2_ref[...], wu_ref[...]), 0.0)
    o_ref[...] += _dot((a * a).astype(BF16), wd_ref[...])


def _out_mlp(att, c, x, w_out, n2, w_up, w_down, layer, tm, tf):
    m = x.shape[0]
    row = lambda i, f: (i, 0)
    return pl.pallas_call(
        _out_mlp_kernel,
        grid=(m // tm, D_FF // tf),
        in_specs=[
            pl.BlockSpec((tm, ATT_W), row),
            pl.BlockSpec((tm, CONV_W), row),
            pl.BlockSpec((tm, D_MODEL), row),
            pl.BlockSpec((None, ATT_W, D_MODEL), lambda i, f: (layer, 0, 0), pipeline_mode=pl.Buffered(1)),
            pl.BlockSpec((None, CONV_W, D_MODEL), lambda i, f: (layer, 1, 0), pipeline_mode=pl.Buffered(1)),
            pl.BlockSpec((1, D_MODEL), lambda i, f: (0, 0)),
            pl.BlockSpec((None, D_MODEL, tf), lambda i, f: (layer, 0, f)),
            pl.BlockSpec((None, tf, D_MODEL), lambda i, f: (layer, f, 0)),
        ],
        out_specs=pl.BlockSpec((tm, D_MODEL), row),
        out_shape=jax.ShapeDtypeStruct((m, D_MODEL), F32),
        scratch_shapes=[pltpu.VMEM((tm, D_MODEL), BF16)],
        compiler_params=_cparams(("arbitrary", "arbitrary")),
        name="out_mlp",
    )(att, c, x, w_out, w_out, n2, w_up, w_down)


def _rope_tables(pos):
    half = ROPE_DIM // 2
    inv_freq = ROPE_THETA ** (-(jnp.arange(half, dtype=F32) * 2.0) / ROPE_DIM)
    ang = pos.astype(F32)[:, None] * inv_freq[None, :]
    cos, sin = jnp.cos(ang), jnp.sin(ang)
    t = pos.shape[0]
    z8 = jnp.zeros((t, half), F32)
    rest0 = jnp.zeros((t, HEAD_DIM - ROPE_DIM), F32)
    c = jnp.concatenate([cos, cos, 1.0 + rest0], axis=-1)
    sa = jnp.concatenate([-sin, z8, rest0], axis=-1)
    sb = jnp.concatenate([z8, sin, rest0], axis=-1)
    return tuple(jnp.tile(a, (1, LANES // HEAD_DIM)) for a in (c, sa, sb))


def _lambda_init(layer):
    return 0.8 - 0.6 * math.exp(-0.3 * layer)


def _tile_div(n, want):
    t = want
    while n % t:
        t //= 2
    return t


def kernel(x_prompt, x_sample, cache_k, cache_v, state_conv, meta_tokens, norm1_g, w_in, q_norm_g, k_norm_g,
           lam_q1, lam_k1, lam_q2, lam_k2, attn_norm_g, conv_w, conv_b, conv_ln_g, conv_ln_b, w_out, norm2_g,
           w_up, w_down):
    assert DEC_SEQ == SLOT and N_META <= SLOT and SEQ % CHUNK == 0 and N_META <= HIST
    slots = DEC_BATCH + 1
    ms = slots * SLOT
    mf = BATCH * SEQ
    nd = DEC_BATCH * SLOT

    xf = x_prompt.reshape(mf, D_MODEL)
    meta_slot = jnp.concatenate([meta_tokens.astype(F32), jnp.zeros((SLOT - N_META, D_MODEL), F32)], axis=0)
    xs = jnp.concatenate([x_sample.reshape(nd, D_MODEL), meta_slot], axis=0)

    pos_f = N_META + jnp.arange(SEQ, dtype=jnp.int32)
    pos_s = jnp.concatenate([
        jnp.tile(N_META + PAST_LEN + jnp.arange(DEC_SEQ, dtype=jnp.int32), DEC_BATCH),
        jnp.arange(N_META, dtype=jnp.int32), jnp.zeros((SLOT - N_META,), jnp.int32)])
    tab_f = _rope_tables(pos_f)
    tab_s = _rope_tables(pos_s)

    grp = jnp.arange(MXU_W) // HEAD_DIM
    p256 = jnp.where(grp[:, None] == grp[None, :], 1.0 / HEAD_DIM, 0.0).astype(BF16)

    w_in_b = w_in.astype(BF16)
    w_out_b = w_out.astype(BF16)
    w_up_b = w_up.astype(BF16)
    w_down_b = w_down.astype(BF16)
    ck = cache_k.reshape(DEPTH, DEC_BATCH, PAST_LEN * N_HEADS, V_DIM)
    cv = cache_v.reshape(DEPTH, DEC_BATCH, PAST_LEN * N_HEADS, V_DIM)

    tm_f = _tile_div(SEQ, 512)
    tm_s = ms // 2
    tq = _tile_div(SEQ, 256)

    k_p, v_p, c_p, k_s, v_s, c_s = [], [], [], [], [], []
    for l in range(DEPTH):
        lam0 = _lambda_init(l)
        lam_p = tuple(a[l].reshape(1, HEAD_DIM) for a in (lam_q1, lam_k1, lam_q2, lam_k2))
        g1 = norm1_g[l].reshape(1, D_MODEL)
        gq = jnp.tile(q_norm_g[l] * (ATTN_SCALE * math.log2(math.e)), LANES // HEAD_DIM).reshape(1, LANES)
        gk = jnp.tile(k_norm_g[l], LANES // HEAD_DIM).reshape(1, LANES)
        gn = attn_norm_g[l].reshape(1, V_DIM)
        cw, cb = conv_w[l], conv_b[l].reshape(1, CONV_W)
        lg, lb = conv_ln_g[l].reshape(1, CONV_W), conv_ln_b[l].reshape(1, CONV_W)
        n2 = norm2_g[l].reshape(1, D_MODEL)

        qs, ks, vs, kbs, vbs, gs = _in_proj(xs, g1, w_in_b, l, p256, gq, gk, *tab_s, tm_s)
        g_meta = gs[nd:nd + N_META]
        hist_f = jnp.concatenate([jnp.zeros((HIST - N_META, CONV_W), F32), g_meta], axis=0)
        qf, kf, vf, kbf, vbf, c_f, g_tail = _in_proj(xf, g1, w_in_b, l, p256, gq, gk, *tab_f, tm_f,
                                                     conv=(hist_f, cw, cb, lg, lb), seq=SEQ)

        att_f = _attn_frames(lam_p, qf, kbf, vbf, kbs, vbs, gn, lam0, tq)
        att_s = _attn_small(lam_p, qs, kbs, vbs, ck, cv, l, gn, lam0)

        hist_s = jnp.concatenate([
            jnp.pad(state_conv[l].astype(F32), ((0, 0), (HIST - (CONV_K - 1), 0), (0, 0))),
            jnp.zeros((1, HIST, CONV_W), F32)], axis=0)
        c_sm = _conv_tail(hist_s, gs.reshape(slots, SLOT, CONV_W), cw, cb, lg, lb).reshape(ms, CONV_W)

        xf = _out_mlp(att_f, c_f, xf, w_out_b, n2, w_up_b, w_down_b, l, tm_f, 1024)
        xs = _out_mlp(att_s, c_sm, xs, w_out_b, n2, w_up_b, w_down_b, l, tm_s, 1024)

        k_meta = jnp.broadcast_to(ks[nd:nd + N_META][None], (BATCH, N_META, ATT_W))
        v_meta = jnp.broadcast_to(vs[nd:nd + N_META][None], (BATCH, N_META, ATT_W))
        k_p.append(jnp.concatenate([k_meta, kf.reshape(BATCH, SEQ, ATT_W)], axis=1))
        v_p.append(jnp.concatenate([v_meta, vf.reshape(BATCH, SEQ, ATT_W)], axis=1))
        g_all = jnp.concatenate([jnp.broadcast_to(hist_f[None], (BATCH, HIST, CONV_W)), g_tail], axis=1)
        c_p.append(g_all[:, -(CONV_K - 1):])
        k_s.append(ks[:nd].reshape(DEC_BATCH, DEC_SEQ, ATT_W))
        v_s.append(vs[:nd].reshape(DEC_BATCH, DEC_SEQ, ATT_W))
        gs_all = jnp.concatenate([hist_s[:DEC_BATCH], gs[:nd].reshape(DEC_BATCH, DEC_SEQ, CONV_W)], axis=1)
        c_s.append(gs_all[:, -(CONV_K - 1):])

    lp = N_META + SEQ
    y_prompt = xf.reshape(BATCH, SEQ, D_MODEL)
    y_sample = xs[:nd].reshape(DEC_BATCH, DEC_SEQ, D_MODEL)
    return (y_prompt, y_sample,
            jnp.stack(k_p).reshape(DEPTH, BATCH, lp, N_HEADS, V_DIM),
            jnp.stack(v_p).reshape(DEPTH, BATCH, lp, N_HEADS, V_DIM),
            jnp.stack(c_p),
            jnp.stack(k_s).reshape(DEPTH, DEC_BATCH, DEC_SEQ, N_HEADS, V_DIM),
            jnp.stack(v_s).reshape(DEPTH, DEC_BATCH, DEC_SEQ, N_HEADS, V_DIM),
            jnp.stack(c_s))
```

```python
import functools
import math

import jax
import jax.numpy as jnp
from jax import lax
from jax.experimental import pallas as pl
from jax.experimental.pallas import tpu as pltpu

F32 = jnp.float32
BF16 = jnp.bfloat16

D_MODEL = 2048
BATCH = 16
SEQ = 2048
DEPTH = 4
DEC_BATCH = 16
DEC_SEQ = 64
PAST_LEN = 2048
CHUNK = 64
N_META = 16
N_HEADS = 8
HEAD_DIM = 64
V_DIM = 2 * HEAD_DIM
ATT_W = N_HEADS * V_DIM
CONV_W = D_MODEL - ATT_W
IN_W = 3 * ATT_W + 2 * CONV_W
CONV_K = 31
ROPE_DIM = HEAD_DIM // 4
ROPE_THETA = 500000.0
D_FF = 4 * D_MODEL
EPS = 1e-6
NEG = -1e30
ATTN_SCALE = HEAD_DIM ** -0.5

SLOT = DEC_SEQ
HIST = 32
LANES = 128
MXU_W = 256
VMEM_LIMIT = 60 * 1024 * 1024


def _cparams(sem):
    return pltpu.CompilerParams(dimension_semantics=sem, vmem_limit_bytes=VMEM_LIMIT)


def _nt_dot(a, b):
    return lax.dot_general(a, b, (((1,), (1,)), ((), ())), preferred_element_type=F32)


def _dot(a, b):
    return jnp.dot(a, b, preferred_element_type=F32)


CONV_ROWS = 32
CONV_UNROLL = 4


def _conv_cols(gbuf_ref, rot_ref, y_ref, cw_ref, cb_ref, tm):
    off = HIST - (CONV_K - 1)
    win = HIST + tm

    def col_body(ci, carry):
        c0 = pl.multiple_of(ci * LANES, LANES)
        xcol = gbuf_ref[:, pl.ds(c0, LANES)]
        for res in range(1, 8):
            rot_ref[res - 1] = pltpu.roll(xcol, win - res, 0)

        def row_body(rb, carry2):
            r0 = pl.multiple_of(rb * CONV_ROWS, CONV_ROWS)
            acc = jnp.zeros((CONV_ROWS, LANES), F32)
            for s in range(off, off + CONV_K):
                res = s % 8
                rows = pl.ds(r0 + (s - res), CONV_ROWS)
                src = gbuf_ref[rows, pl.ds(c0, LANES)] if res == 0 else rot_ref[res - 1, rows, :]
                acc = acc + src * cw_ref[s - off:s - off + 1, pl.ds(c0, LANES)]
            y_ref[pl.ds(r0, CONV_ROWS), pl.ds(c0, LANES)] = acc + cb_ref[:, pl.ds(c0, LANES)]
            return carry2

        lax.fori_loop(0, tm // CONV_ROWS, row_body, 0, unroll=min(CONV_UNROLL, tm // CONV_ROWS))
        return carry

    lax.fori_loop(0, CONV_W // LANES, col_body, 0)


def _ln_silu(y_ref, lg_ref, lb_ref, c_ref):
    y = y_ref[...]
    yc = y - jnp.mean(y, axis=-1, keepdims=True)
    yn = yc * lax.rsqrt(jnp.mean(yc * yc, axis=-1, keepdims=True) + EPS)
    yn = yn * lg_ref[...] + lb_ref[...]
    c_ref[...] = (yn * jax.nn.sigmoid(yn)).astype(c_ref.dtype)


def _in_proj_kernel(x_ref, g1_ref, w_ref, p_ref, gq_ref, gk_ref, c_ref, sa_ref, sb_ref, *rest, tm, fuse_conv, tiles_per_seq):
    if fuse_conv:
        (hist_ref, cw_ref, cb_ref, lg_ref, lb_ref,
         q_ref, k_ref, v_ref, kb_ref, vb_ref, co_ref, gt_ref, gbuf_ref, rot_ref, y_ref) = rest
    else:
        q_ref, k_ref, v_ref, kb_ref, vb_ref, g_ref = rest
    x = x_ref[...]
    ms = jnp.mean(x * x, axis=-1, keepdims=True)
    xn = (x * lax.rsqrt(ms + EPS) * g1_ref[...]).astype(BF16)

    def qk_norm_rope(z, gain_ref, out_refs):
        for s in range(ATT_W // MXU_W):
            zz = z[:, s * MXU_W:(s + 1) * MXU_W]
            ms = _dot((zz * zz).astype(BF16), p_ref[...])
            y = zz * lax.rsqrt(ms + EPS)
            for hh in range(MXU_W // LANES):
                lo = s * MXU_W + hh * LANES
                yh = y[:, hh * LANES:(hh + 1) * LANES] * gain_ref[...]
                r = (yh * c_ref[...]
                     + pltpu.roll(yh, LANES - ROPE_DIM // 2, 1) * sa_ref[...]
                     + pltpu.roll(yh, ROPE_DIM // 2, 1) * sb_ref[...])
                for o in out_refs:
                    o[:, lo:lo + LANES] = r.astype(o.dtype)

    a = _dot(xn, w_ref[:, 3 * ATT_W:3 * ATT_W + CONV_W])
    gate = _dot(xn, w_ref[:, 3 * ATT_W + CONV_W:3 * ATT_W + 2 * CONV_W])
    g = a * jax.nn.sigmoid(gate)
    if fuse_conv:
        first = pl.program_id(0) % tiles_per_seq == 0

        @pl.when(first)
        def _():
            gbuf_ref[0:HIST, :] = hist_ref[...]

        @pl.when(jnp.logical_not(first))
        def _():
            gbuf_ref[0:HIST, :] = gbuf_ref[tm:tm + HIST, :]

        gbuf_ref[HIST:HIST + tm, :] = g
        gt_ref[...] = g[tm - HIST:tm, :]
    else:
        g_ref[...] = g
    qk_norm_rope(_dot(xn, w_ref[:, 0:ATT_W]), gq_ref, (q_ref,))
    qk_norm_rope(_dot(xn, w_ref[:, ATT_W:2 * ATT_W]), gk_ref, (k_ref, kb_ref))
    v = _dot(xn, w_ref[:, 2 * ATT_W:3 * ATT_W])
    v_ref[...] = v
    vb_ref[...] = v.astype(BF16)
    if fuse_conv:
        _conv_cols(gbuf_ref, rot_ref, y_ref, cw_ref, cb_ref, tm)
        _ln_silu(y_ref, lg_ref, lb_ref, co_ref)


def _in_proj(x, g1, w_in, layer, p256, gq, gk, c_t, sa_t, sb_t, tm, conv=None, seq=None):
    m = x.shape[0]
    nt = c_t.shape[0] // tm
    row = lambda i: (i, 0)
    const = lambda i: (0, 0)
    tab = lambda i: (i % nt, 0)
    f32_out = jax.ShapeDtypeStruct((m, ATT_W), F32)
    bf_out = jax.ShapeDtypeStruct((m, ATT_W), BF16)
    in_specs = [
        pl.BlockSpec((tm, D_MODEL), row),
        pl.BlockSpec((1, D_MODEL), const),
        pl.BlockSpec((None, D_MODEL, IN_W), lambda i: (layer, 0, 0), pipeline_mode=pl.Buffered(1)),
        pl.BlockSpec((MXU_W, MXU_W), const),
        pl.BlockSpec((1, LANES), const),
        pl.BlockSpec((1, LANES), const),
        pl.BlockSpec((tm, LANES), tab),
        pl.BlockSpec((tm, LANES), tab),
        pl.BlockSpec((tm, LANES), tab),
    ]
    out_specs = [pl.BlockSpec((tm, ATT_W), row)] * 5 + [pl.BlockSpec((tm, CONV_W), row)]
    args = [x, g1, w_in, p256, gq, gk, c_t, sa_t, sb_t]
    if conv is None:
        out_shape = [bf_out, f32_out, f32_out, bf_out, bf_out, jax.ShapeDtypeStruct((m, CONV_W), F32)]
        scratch = []
        tps = 1
    else:
        tps = seq // tm
        vec = pl.BlockSpec((1, CONV_W), const)
        in_specs += [pl.BlockSpec((HIST, CONV_W), const), pl.BlockSpec((CONV_K, CONV_W), const), vec, vec, vec]
        args += list(conv)
        out_specs = out_specs + [pl.BlockSpec((None, HIST, CONV_W), lambda i: (i // tps, 0, 0))]
        out_shape = [bf_out, f32_out, f32_out, bf_out, bf_out, jax.ShapeDtypeStruct((m, CONV_W), BF16),
                     jax.ShapeDtypeStruct((m // seq, HIST, CONV_W), F32)]
        scratch = [pltpu.VMEM((HIST + tm, CONV_W), F32), pltpu.VMEM((7, HIST + tm, LANES), F32),
                   pltpu.VMEM((tm, CONV_W), F32)]
    return pl.pallas_call(
        functools.partial(_in_proj_kernel, tm=tm, fuse_conv=conv is not None, tiles_per_seq=tps),
        grid=(m // tm,),
        in_specs=in_specs,
        out_specs=out_specs,
        out_shape=out_shape,
        scratch_shapes=scratch,
        compiler_params=_cparams(("arbitrary",)),
        name="in_proj",
    )(*args)


def _lam(lq1_ref, lk1_ref, lq2_ref, lk2_ref, lam0):
    a = jnp.sum(lq1_ref[...] * lk1_ref[...], axis=-1, keepdims=True)
    b = jnp.sum(lq2_ref[...] * lk2_ref[...], axis=-1, keepdims=True)
    return jnp.exp(a) - jnp.exp(b) + lam0


def _split_q(q):
    lane = lax.broadcasted_iota(jnp.int32, q.shape, 1)
    zero = jnp.zeros_like(q)
    return jnp.where(lane < HEAD_DIM, q, zero), jnp.where(lane >= HEAD_DIM, q, zero)


def _pad_meta(ref):
    return jnp.concatenate([ref[0:N_META, :], jnp.zeros((LANES - N_META, LANES), BF16)], axis=0)


def _meta_scores(qc, kmp):
    s = _nt_dot(qc, kmp)
    lane = lax.broadcasted_iota(jnp.int32, s.shape, 1)
    return jnp.where(lane < N_META, s, NEG)


def _softmax_pv(pieces):
    m = None
    for s, _ in pieces:
        r = jnp.max(s, axis=-1, keepdims=True)
        m = r if m is None else jnp.maximum(m, r)
    l = None
    o = None
    for s, v in pieces:
        e = jnp.exp2(s - m)
        ls = jnp.sum(e, axis=-1, keepdims=True)
        os_ = _dot(e.astype(BF16), v)
        l = ls if l is None else l + ls
        o = os_ if o is None else o + os_
    return o / l


def _head_out(o1, o2, lam, gn_ref, lam0):
    o = o1 - lam * o2
    y = o * lax.rsqrt(jnp.mean(o * o, axis=-1, keepdims=True) + EPS) * gn_ref[...]
    return y * (1.0 - lam0)


def _attn_frames_kernel(lq1_ref, lk1_ref, lq2_ref, lk2_ref, q_ref, k_ref, v_ref, km_ref, vm_ref,
                        gn_ref, o_ref, vb_ref, q1_ref, q2_ref, s1_ref, s2_ref, *, lam0, tq, seq):
    lam = _lam(lq1_ref, lk1_ref, lq2_ref, lk2_ref, lam0)

    @pl.when((pl.program_id(0) == 0) & (pl.program_id(1) == 0))
    def _():
        vb_ref[:, LANES:2 * LANES] = jnp.ones((LANES + seq, LANES), BF16)

    vb_ref[0:LANES, 0:LANES] = _pad_meta(vm_ref)
    vb_ref[LANES:LANES + seq, 0:LANES] = v_ref[...]
    q1, q2 = _split_q(q_ref[...])
    q1_ref[...] = q1
    q2_ref[...] = q2
    kmp = _pad_meta(km_ref)
    ri = lax.broadcasted_iota(jnp.int32, (tq, tq), 0) // CHUNK
    ci = lax.broadcasted_iota(jnp.int32, (tq, tq), 1) // CHUNK
    diag_ok = ci <= ri

    for i in range(seq // tq):
        lo = i * tq
        outs = []
        for qc_ref, s_ref in ((q1_ref, s1_ref), (q2_ref, s2_ref)):
            qc = qc_ref[lo:lo + tq, :]
            sm = _meta_scores(qc, kmp)
            s_ref[:, 0:LANES] = sm
            mrun = sm
            for j in range(i + 1):
                s = _nt_dot(qc, k_ref[j * tq:(j + 1) * tq, :])
                if j == i:
                    s = jnp.where(diag_ok, s, NEG)
                s_ref[:, LANES + j * tq:LANES + (j + 1) * tq] = s
                for c0 in range(0, tq, LANES):
                    mrun = jnp.maximum(mrun, s[:, c0:c0 + LANES])
            m = jnp.max(mrun, axis=-1, keepdims=True)
            nk = LANES + (i + 1) * tq
            acc = _dot(jnp.exp2(s_ref[:, 0:nk] - m).astype(BF16), vb_ref[0:nk, :])
            outs.append(acc[:, 0:LANES] / acc[:, LANES:2 * LANES])
        o_ref[lo:lo + tq, :] = _head_out(outs[0], outs[1], lam, gn_ref, lam0).astype(o_ref.dtype)


def _attn_frames(lam_p, q, kb, vb, kb_small, vb_small, gn, lam0, tq):
    nb = q.shape[0] // SEQ
    lam_spec = pl.BlockSpec((1, HEAD_DIM), lambda b, h: (0, 0))
    blk = pl.BlockSpec((SEQ, LANES), lambda b, h: (b, h))
    meta = pl.BlockSpec((SLOT, LANES), lambda b, h: (DEC_BATCH, h))
    return pl.pallas_call(
        functools.partial(_attn_frames_kernel, lam0=lam0, tq=tq, seq=SEQ),
        grid=(nb, N_HEADS),
        in_specs=[lam_spec] * 4 + [blk, blk, blk, meta, meta, pl.BlockSpec((1, LANES), lambda b, h: (0, 0))],
        out_specs=blk,
        out_shape=jax.ShapeDtypeStruct(q.shape, BF16),
        scratch_shapes=[pltpu.VMEM((LANES + SEQ, 2 * LANES), BF16),
                        pltpu.VMEM((SEQ, LANES), BF16), pltpu.VMEM((SEQ, LANES), BF16),
                        pltpu.VMEM((tq, LANES + SEQ), F32), pltpu.VMEM((tq, LANES + SEQ), F32)],
        compiler_params=_cparams(("arbitrary", "arbitrary")),
        name="attn_frames",
    )(*lam_p, q, kb, vb, kb_small, vb_small, gn)


def _attn_small_kernel(lq1_ref, lk1_ref, lq2_ref, lk2_ref, q_ref, k_ref, v_ref, km_ref, vm_ref,
                       ck_ref, cv_ref, gn_ref, o_ref, *, lam0):
    s = pl.program_id(0)
    lam = _lam(lq1_ref, lk1_ref, lq2_ref, lk2_ref, lam0)

    def head(h, with_cache):
        cols = slice(h * LANES, (h + 1) * LANES)
        q1, q2 = _split_q(q_ref[:, cols])
        qq = jnp.concatenate([q1, q2], axis=0)
        kmp = jnp.concatenate([km_ref[0:N_META, cols], jnp.zeros((LANES - N_META, LANES), BF16)], axis=0)
        vmp = jnp.concatenate([vm_ref[0:N_META, cols], jnp.zeros((LANES - N_META, LANES), BF16)], axis=0)
        pieces = [(_meta_scores(qq, kmp), vmp)]
        if with_cache:
            rows = pl.ds(h, PAST_LEN, stride=N_HEADS)
            pieces.append((_nt_dot(qq, ck_ref[rows, :].astype(BF16)), cv_ref[rows, :].astype(BF16)))
            pieces.append((_nt_dot(qq, k_ref[:, cols]), v_ref[:, cols]))
        o = _softmax_pv(pieces)
        o_ref[:, cols] = _head_out(o[0:SLOT], o[SLOT:2 * SLOT], lam, gn_ref, lam0).astype(o_ref.dtype)

    @pl.when(s < DEC_BATCH)
    def _():
        for h in range(N_HEADS):
            head(h, True)

    @pl.when(s == DEC_BATCH)
    def _():
        for h in range(N_HEADS):
            head(h, False)


def _attn_small(lam_p, q, k, v, cache_k, cache_v, layer, gn, lam0):
    slots = q.shape[0] // SLOT
    lam_spec = pl.BlockSpec((1, HEAD_DIM), lambda s: (0, 0))
    blk = pl.BlockSpec((SLOT, ATT_W), lambda s: (s, 0))
    meta = pl.BlockSpec((SLOT, ATT_W), lambda s: (DEC_BATCH, 0))
    cache = pl.BlockSpec((None, None, PAST_LEN * N_HEADS, V_DIM),
                         lambda s: (layer, jnp.minimum(s, DEC_BATCH - 1), 0, 0))
    return pl.pallas_call(
        functools.partial(_attn_small_kernel, lam0=lam0),
        grid=(slots,),
        in_specs=[lam_spec] * 4 + [blk, blk, blk, meta, meta, cache, cache,
                                   pl.BlockSpec((1, V_DIM), lambda s: (0, 0))],
        out_specs=blk,
        out_shape=jax.ShapeDtypeStruct(q.shape, BF16),
        compiler_params=_cparams(("arbitrary",)),
        name="attn_small",
    )(*lam_p, q, k, v, k, v, cache_k, cache_v, gn)


def _conv_kernel(hist_ref, g_ref, w_ref, b_ref, lg_ref, lb_ref, c_ref, gbuf_ref, rot_ref, y_ref, *, tt):
    gbuf_ref[0:HIST, :] = hist_ref[...]
    gbuf_ref[HIST:HIST + tt, :] = g_ref[...]
    _conv_cols(gbuf_ref, rot_ref, y_ref, w_ref, b_ref, tt)
    _ln_silu(y_ref, lg_ref, lb_ref, c_ref)


def _conv_tail(hist, g, w, b, lg, lb):
    nb, tt, _ = g.shape
    vec = pl.BlockSpec((1, CONV_W), lambda bi: (0, 0))
    return pl.pallas_call(
        functools.partial(_conv_kernel, tt=tt),
        grid=(nb,),
        in_specs=[
            pl.BlockSpec((None, HIST, CONV_W), lambda bi: (bi, 0, 0)),
            pl.BlockSpec((None, tt, CONV_W), lambda bi: (bi, 0, 0)),
            pl.BlockSpec((CONV_K, CONV_W), lambda bi: (0, 0)),
            vec, vec, vec,
        ],
        out_specs=pl.BlockSpec((None, tt, CONV_W), lambda bi: (bi, 0, 0)),
        out_shape=jax.ShapeDtypeStruct(g.shape, BF16),
        scratch_shapes=[pltpu.VMEM((HIST + tt, CONV_W), F32), pltpu.VMEM((7, HIST + tt, LANES), F32),
                        pltpu.VMEM((tt, CONV_W), F32)],
        compiler_params=_cparams(("arbitrary",)),
        name="conv_tail",
    )(hist, g, w, b, lg, lb)


def _out_mlp_kernel(att_ref, c_ref, x_ref, wa_ref, wc_ref, n2_ref, wu_ref, wd_ref, o_ref, h2_ref):
    f = pl.program_id(1)

    @pl.when(f == 0)
    def _():
        x1 = x_ref[...] + _dot(att_ref[...], wa_ref[...]) + _dot(c_ref[...], wc_ref[...])
        o_ref[...] = x1
        ms = jnp.mean(x1 * x1, axis=-1, keepdims=True)
        h2_ref[...] = (x1 * lax.rsqrt(ms + EPS) * n2_ref[...]).astype(BF16)

    a = jnp.maximum(_dot(h2_ref[...], wu_ref[...]), 0.0)
    o_ref[...] += _dot((a * a).astype(BF16), wd_ref[...])


def _out_mlp(att, c, x, w_out, n2, w_up, w_down, layer, tm, tf):
    m = x.shape[0]
    row = lambda i, f: (i, 0)
    return pl.pallas_call(
        _out_mlp_kernel,
        grid=(m // tm, D_FF // tf),
        in_specs=[
            pl.BlockSpec((tm, ATT_W), row),
            pl.BlockSpec((tm, CONV_W), row),
            pl.BlockSpec((tm, D_MODEL), row),
            pl.BlockSpec((None, ATT_W, D_MODEL), lambda i, f: (layer, 0, 0), pipeline_mode=pl.Buffered(1)),
            pl.BlockSpec((None, CONV_W, D_MODEL), lambda i, f: (layer, 1, 0), pipeline_mode=pl.Buffered(1)),
            pl.BlockSpec((1, D_MODEL), lambda i, f: (0, 0)),
            pl.BlockSpec((None, D_MODEL, tf), lambda i, f: (layer, 0, f)),
            pl.BlockSpec((None, tf, D_MODEL), lambda i, f: (layer, f, 0)),
        ],
        out_specs=pl.BlockSpec((tm, D_MODEL), row),
        out_shape=jax.ShapeDtypeStruct((m, D_MODEL), F32),
        scratch_shapes=[pltpu.VMEM((tm, D_MODEL), BF16)],
        compiler_params=_cparams(("arbitrary", "arbitrary")),
        name="out_mlp",
    )(att, c, x, w_out, w_out, n2, w_up, w_down)


def _rope_tables(pos):
    half = ROPE_DIM // 2
    inv_freq = ROPE_THETA ** (-(jnp.arange(half, dtype=F32) * 2.0) / ROPE_DIM)
    ang = pos.astype(F32)[:, None] * inv_freq[None, :]
    cos, sin = jnp.cos(ang), jnp.sin(ang)
    t = pos.shape[0]
    z8 = jnp.zeros((t, half), F32)
    rest0 = jnp.zeros((t, HEAD_DIM - ROPE_DIM), F32)
    c = jnp.concatenate([cos, cos, 1.0 + rest0], axis=-1)
    sa = jnp.concatenate([-sin, z8, rest0], axis=-1)
    sb = jnp.concatenate([z8, sin, rest0], axis=-1)
    return tuple(jnp.tile(a, (1, LANES // HEAD_DIM)) for a in (c, sa, sb))


def _lambda_init(layer):
    return 0.8 - 0.6 * math.exp(-0.3 * layer)


def _tile_div(n, want):
    t = want
    while n % t:
        t //= 2
    return t


def kernel(x_prompt, x_sample, cache_k, cache_v, state_conv, meta_tokens, norm1_g, w_in, q_norm_g, k_norm_g,
           lam_q1, lam_k1, lam_q2, lam_k2, attn_norm_g, conv_w, conv_b, conv_ln_g, conv_ln_b, w_out, norm2_g,
           w_up, w_down):
    assert DEC_SEQ == SLOT and N_META <= SLOT and SEQ % CHUNK == 0 and N_META <= HIST
    slots = DEC_BATCH + 1
    ms = slots * SLOT
    mf = BATCH * SEQ
    nd = DEC_BATCH * SLOT

    xf = x_prompt.reshape(mf, D_MODEL)
    meta_slot = jnp.concatenate([meta_tokens.astype(F32), jnp.zeros((SLOT - N_META, D_MODEL), F32)], axis=0)
    xs = jnp.concatenate([x_sample.reshape(nd, D_MODEL), meta_slot], axis=0)

    pos_f = N_META + jnp.arange(SEQ, dtype=jnp.int32)
    pos_s = jnp.concatenate([
        jnp.tile(N_META + PAST_LEN + jnp.arange(DEC_SEQ, dtype=jnp.int32), DEC_BATCH),
        jnp.arange(N_META, dtype=jnp.int32), jnp.zeros((SLOT - N_META,), jnp.int32)])
    tab_f = _rope_tables(pos_f)
    tab_s = _rope_tables(pos_s)

    grp = jnp.arange(MXU_W) // HEAD_DIM
    p256 = jnp.where(grp[:, None] == grp[None, :], 1.0 / HEAD_DIM, 0.0).astype(BF16)

    w_in_b = w_in.astype(BF16)
    w_out_b = w_out.astype(BF16)
    w_up_b = w_up.astype(BF16)
    w_down_b = w_down.astype(BF16)
    ck = cache_k.reshape(DEPTH, DEC_BATCH, PAST_LEN * N_HEADS, V_DIM)
    cv = cache_v.reshape(DEPTH, DEC_BATCH, PAST_LEN * N_HEADS, V_DIM)

    tm_f = _tile_div(SEQ, 512)
    tm_s = ms // 2
    tq = _tile_div(SEQ, 256)

    k_p, v_p, c_p, k_s, v_s, c_s = [], [], [], [], [], []
    for l in range(DEPTH):
        lam0 = _lambda_init(l)
        lam_p = tuple(a[l].reshape(1, HEAD_DIM) for a in (lam_q1, lam_k1, lam_q2, lam_k2))
        g1 = norm1_g[l].reshape(1, D_MODEL)
        gq = jnp.tile(q_norm_g[l] * (ATTN_SCALE * math.log2(math.e)), LANES // HEAD_DIM).reshape(1, LANES)
        gk = jnp.tile(k_norm_g[l], LANES // HEAD_DIM).reshape(1, LANES)
        gn = attn_norm_g[l].reshape(1, V_DIM)
        cw, cb = conv_w[l], conv_b[l].reshape(1, CONV_W)
        lg, lb = conv_ln_g[l].reshape(1, CONV_W), conv_ln_b[l].reshape(1, CONV_W)
        n2 = norm2_g[l].reshape(1, D_MODEL)

        qs, ks, vs, kbs, vbs, gs = _in_proj(xs, g1, w_in_b, l, p256, gq, gk, *tab_s, tm_s)
        g_meta = gs[nd:nd + N_META]
        hist_f = jnp.concatenate([jnp.zeros((HIST - N_META, CONV_W), F32), g_meta], axis=0)
        qf, kf, vf, kbf, vbf, c_f, g_tail = _in_proj(xf, g1, w_in_b, l, p256, gq, gk, *tab_f, tm_f,
                                                     conv=(hist_f, cw, cb, lg, lb), seq=SEQ)

        att_f = _attn_frames(lam_p, qf, kbf, vbf, kbs, vbs, gn, lam0, tq)
        att_s = _attn_small(lam_p, qs, kbs, vbs, ck, cv, l, gn, lam0)

        hist_s = jnp.concatenate([
            jnp.pad(state_conv[l].astype(F32), ((0, 0), (HIST - (CONV_K - 1), 0), (0, 0))),
            jnp.zeros((1, HIST, CONV_W), F32)], axis=0)
        c_sm = _conv_tail(hist_s, gs.reshape(slots, SLOT, CONV_W), cw, cb, lg, lb).reshape(ms, CONV_W)

        xf = _out_mlp(att_f, c_f, xf, w_out_b, n2, w_up_b, w_down_b, l, tm_f, 1024)
        xs = _out_mlp(att_s, c_sm, xs, w_out_b, n2, w_up_b, w_down_b, l, tm_s, 1024)

        k_meta = jnp.broadcast_to(ks[nd:nd + N_META][None], (BATCH, N_META, ATT_W))
        v_meta = jnp.broadcast_to(vs[nd:nd + N_META][None], (BATCH, N_META, ATT_W))
        k_p.append(jnp.concatenate([k_meta, kf.reshape(BATCH, SEQ, ATT_W)], axis=1))
        v_p.append(jnp.concatenate([v_meta, vf.reshape(BATCH, SEQ, ATT_W)], axis=1))
        g_all = jnp.concatenate([jnp.broadcast_to(hist_f[None], (BATCH, HIST, CONV_W)), g_tail], axis=1)
        c_p.append(g_all[:, -(CONV_K - 1):])
        k_s.append(ks[:nd].reshape(DEC_BATCH, DEC_SEQ, ATT_W))
        v_s.append(vs[:nd].reshape(DEC_BATCH, DEC_SEQ, ATT_W))
        gs_all = jnp.concatenate([hist_s[:DEC_BATCH], gs[:nd].reshape(DEC_BATCH, DEC_SEQ, CONV_W)], axis=1)
        c_s.append(gs_all[:, -(CONV_K - 1):])

    lp = N_META + SEQ
    y_prompt = xf.reshape(BATCH, SEQ, D_MODEL)
    y_sample = xs[:nd].reshape(DEC_BATCH, DEC_SEQ, D_MODEL)
    return (y_prompt, y_sample,
            jnp.stack(k_p).reshape(DEPTH, BATCH, lp, N_HEADS, V_DIM),
            jnp.stack(v_p).reshape(DEPTH, BATCH, lp, N_HEADS, V_DIM),
            jnp.stack(c_p),
            jnp.stack(k_s).reshape(DEPTH, DEC_BATCH, DEC_SEQ, N_HEADS, V_DIM),
            jnp.stack(v_s).reshape(DEPTH, DEC_BATCH, DEC_SEQ, N_HEADS, V_DIM),
            jnp.stack(c_s))
```

```python
import functools
import math

import jax
import jax.numpy as jnp
from jax import lax
from jax.experimental import pallas as pl
from jax.experimental.pallas import tpu as pltpu

F32 = jnp.float32
BF16 = jnp.bfloat16

D_MODEL = 2048
BATCH = 16
SEQ = 2048
DEPTH = 4
DEC_BATCH = 16
DEC_SEQ = 64
PAST_LEN = 2048
CHUNK = 64
N_META = 16
N_HEADS = 8
HEAD_DIM = 64
V_DIM = 2 * HEAD_DIM
ATT_W = N_HEADS * V_DIM
CONV_W = D_MODEL - ATT_W
IN_W = 3 * ATT_W + 2 * CONV_W
CONV_K = 31
ROPE_DIM = HEAD_DIM // 4
ROPE_THETA = 500000.0
D_FF = 4 * D_MODEL
EPS = 1e-6
NEG = -1e30
ATTN_SCALE = HEAD_DIM ** -0.5

SLOT = DEC_SEQ
HIST = 32
LANES = 128
MXU_W = 256
VMEM_LIMIT = 60 * 1024 * 1024


def _cparams(sem):
    return pltpu.CompilerParams(dimension_semantics=sem, vmem_limit_bytes=VMEM_LIMIT)


def _nt_dot(a, b):
    return lax.dot_general(a, b, (((1,), (1,)), ((), ())), preferred_element_type=F32)


def _dot(a, b):
    return jnp.dot(a, b, preferred_element_type=F32)


CONV_ROWS = 32
CONV_UNROLL = 4


def _conv_cols(gbuf_ref, rot_ref, y_ref, cw_ref, cb_ref, tm):
    off = HIST - (CONV_K - 1)
    win = HIST + tm

    def col_body(ci, carry):
        c0 = pl.multiple_of(ci * LANES, LANES)
        xcol = gbuf_ref[:, pl.ds(c0, LANES)]
        for res in range(1, 8):
            rot_ref[res - 1] = pltpu.roll(xcol, win - res, 0)

        def row_body(rb, carry2):
            r0 = pl.multiple_of(rb * CONV_ROWS, CONV_ROWS)
            acc = jnp.zeros((CONV_ROWS, LANES), F32)
            for s in range(off, off + CONV_K):
                res = s % 8
                rows = pl.ds(r0 + (s - res), CONV_ROWS)
                src = gbuf_ref[rows, pl.ds(c0, LANES)] if res == 0 else rot_ref[res - 1, rows, :]
                acc = acc + src * cw_ref[s - off:s - off + 1, pl.ds(c0, LANES)]
            y_ref[pl.ds(r0, CONV_ROWS), pl.ds(c0, LANES)] = acc + cb_ref[:, pl.ds(c0, LANES)]
            return carry2

        lax.fori_loop(0, tm // CONV_ROWS, row_body, 0, unroll=min(CONV_UNROLL, tm // CONV_ROWS))
        return carry

    lax.fori_loop(0, CONV_W // LANES, col_body, 0)


def _ln_silu(y_ref, lg_ref, lb_ref, c_ref):
    y = y_ref[...]
    yc = y - jnp.mean(y, axis=-1, keepdims=True)
    yn = yc * lax.rsqrt(jnp.mean(yc * yc, axis=-1, keepdims=True) + EPS)
    yn = yn * lg_ref[...] + lb_ref[...]
    c_ref[...] = (yn * jax.nn.sigmoid(yn)).astype(c_ref.dtype)


def _in_proj_kernel(x_ref, g1_ref, w_ref, p_ref, gq_ref, gk_ref, c_ref, sa_ref, sb_ref, *rest, tm, fuse_conv, tiles_per_seq):
    if fuse_conv:
        (hist_ref, cw_ref, cb_ref, lg_ref, lb_ref,
         q_ref, k_ref, v_ref, kb_ref, vb_ref, co_ref, gt_ref, gbuf_ref, rot_ref, y_ref) = rest
    else:
        q_ref, k_ref, v_ref, kb_ref, vb_ref, g_ref = rest
    x = x_ref[...]
    ms = jnp.mean(x * x, axis=-1, keepdims=True)
    xn = (x * lax.rsqrt(ms + EPS) * g1_ref[...]).astype(BF16)

    def qk_norm_rope(z, gain_ref, out_refs):
        for s in range(ATT_W // MXU_W):
            zz = z[:, s * MXU_W:(s + 1) * MXU_W]
            ms = _dot((zz * zz).astype(BF16), p_ref[...])
            y = zz * lax.rsqrt(ms + EPS)
            for hh in range(MXU_W // LANES):
                lo = s * MXU_W + hh * LANES
                yh = y[:, hh * LANES:(hh + 1) * LANES] * gain_ref[...]
                r = (yh * c_ref[...]
                     + pltpu.roll(yh, LANES - ROPE_DIM // 2, 1) * sa_ref[...]
                     + pltpu.roll(yh, ROPE_DIM // 2, 1) * sb_ref[...])
                for o in out_refs:
                    o[:, lo:lo + LANES] = r.astype(o.dtype)

    a = _dot(xn, w_ref[:, 3 * ATT_W:3 * ATT_W + CONV_W])
    gate = _dot(xn, w_ref[:, 3 * ATT_W + CONV_W:3 * ATT_W + 2 * CONV_W])
    g = a * jax.nn.sigmoid(gate)
    if fuse_conv:
        first = pl.program_id(0) % tiles_per_seq == 0

        @pl.when(first)
        def _():
            gbuf_ref[0:HIST, :] = hist_ref[...]

        @pl.when(jnp.logical_not(first))
        def _():
            gbuf_ref[0:HIST, :] = gbuf_ref[tm:tm + HIST, :]

        gbuf_ref[HIST:HIST + tm, :] = g
        gt_ref[...] = g[tm - HIST:tm, :]
    else:
        g_ref[...] = g
    qk_norm_rope(_dot(xn, w_ref[:, 0:ATT_W]), gq_ref, (q_ref,))
    qk_norm_rope(_dot(xn, w_ref[:, ATT_W:2 * ATT_W]), gk_ref, (k_ref, kb_ref))
    v = _dot(xn, w_ref[:, 2 * ATT_W:3 * ATT_W])
    v_ref[...] = v
    vb_ref[...] = v.astype(BF16)
    if fuse_conv:
        _conv_cols(gbuf_ref, rot_ref, y_ref, cw_ref, cb_ref, tm)
        _ln_silu(y_ref, lg_ref, lb_ref, co_ref)


def _in_proj(x, g1, w_in, layer, p256, gq, gk, c_t, sa_t, sb_t, tm, conv=None, seq=None):
    m = x.shape[0]
    nt = c_t.shape[0] // tm
    row = lambda i: (i, 0)
    const = lambda i: (0, 0)
    tab = lambda i: (i % nt, 0)
    f32_out = jax.ShapeDtypeStruct((m, ATT_W), F32)
    bf_out = jax.ShapeDtypeStruct((m, ATT_W), BF16)
    in_specs = [
        pl.BlockSpec((tm, D_MODEL), row),
        pl.BlockSpec((1, D_MODEL), const),
        pl.BlockSpec((None, D_MODEL, IN_W), lambda i: (layer, 0, 0), pipeline_mode=pl.Buffered(1)),
        pl.BlockSpec((MXU_W, MXU_W), const),
        pl.BlockSpec((1, LANES), const),
        pl.BlockSpec((1, LANES), const),
        pl.BlockSpec((tm, LANES), tab),
        pl.BlockSpec((tm, LANES), tab),
        pl.BlockSpec((tm, LANES), tab),
    ]
    out_specs = [pl.BlockSpec((tm, ATT_W), row)] * 5 + [pl.BlockSpec((tm, CONV_W), row)]
    args = [x, g1, w_in, p256, gq, gk, c_t, sa_t, sb_t]
    if conv is None:
        out_shape = [bf_out, f32_out, f32_out, bf_out, bf_out, jax.ShapeDtypeStruct((m, CONV_W), F32)]
        scratch = []
        tps = 1
    else:
        tps = seq // tm
        vec = pl.BlockSpec((1, CONV_W), const)
        in_specs += [pl.BlockSpec((HIST, CONV_W), const), pl.BlockSpec((CONV_K, CONV_W), const), vec, vec, vec]
        args += list(conv)
        out_specs = out_specs + [pl.BlockSpec((None, HIST, CONV_W), lambda i: (i // tps, 0, 0))]
        out_shape = [bf_out, f32_out, f32_out, bf_out, bf_out, jax.ShapeDtypeStruct((m, CONV_W), BF16),
                     jax.ShapeDtypeStruct((m // seq, HIST, CONV_W), F32)]
        scratch = [pltpu.VMEM((HIST + tm, CONV_W), F32), pltpu.VMEM((7, HIST + tm, LANES), F32),
                   pltpu.VMEM((tm, CONV_W), F32)]
    return pl.pallas_call(
        functools.partial(_in_proj_kernel, tm=tm, fuse_conv=conv is not None, tiles_per_seq=tps),
        grid=(m // tm,),
        in_specs=in_specs,
        out_specs=out_specs,
        out_shape=out_shape,
        scratch_shapes=scratch,
        compiler_params=_cparams(("arbitrary",)),
        name="in_proj",
    )(*args)


def _lam(lq1_ref, lk1_ref, lq2_ref, lk2_ref, lam0):
    a = jnp.sum(lq1_ref[...] * lk1_ref[...], axis=-1, keepdims=True)
    b = jnp.sum(lq2_ref[...] * lk2_ref[...], axis=-1, keepdims=True)
    return jnp.exp(a) - jnp.exp(b) + lam0


def _split_q(q):
    lane = lax.broadcasted_iota(jnp.int32, q.shape, 1)
    zero = jnp.zeros_like(q)
    return jnp.where(lane < HEAD_DIM, q, zero), jnp.where(lane >= HEAD_DIM, q, zero)


def _pad_meta(ref):
    return jnp.concatenate([ref[0:N_META, :], jnp.zeros((LANES - N_META, LANES), BF16)], axis=0)


def _meta_scores(qc, kmp):
    s = _nt_dot(qc, kmp)
    lane = lax.broadcasted_iota(jnp.int32, s.shape, 1)
    return jnp.where(lane < N_META, s, NEG)


def _softmax_pv(pieces):
    m = None
    for s, _ in pieces:
        r = jnp.max(s, axis=-1, keepdims=True)
        m = r if m is None else jnp.maximum(m, r)
    l = None
    o = None
    for s, v in pieces:
        e = jnp.exp2(s - m)
        ls = jnp.sum(e, axis=-1, keepdims=True)
        os_ = _dot(e.astype(BF16), v)
        l = ls if l is None else l + ls
        o = os_ if o is None else o + os_
    return o / l


def _head_out(o1, o2, lam, gn_ref, lam0):
    o = o1 - lam * o2
    y = o * lax.rsqrt(jnp.mean(o * o, axis=-1, keepdims=True) + EPS) * gn_ref[...]
    return y * (1.0 - lam0)


def _attn_frames_kernel(lq1_ref, lk1_ref, lq2_ref, lk2_ref, q_ref, k_ref, v_ref, km_ref, vm_ref,
                        gn_ref, o_ref, vb_ref, q1_ref, q2_ref, s1_ref, s2_ref, *, lam0, tq, seq):
    lam = _lam(lq1_ref, lk1_ref, lq2_ref, lk2_ref, lam0)

    @pl.when((pl.program_id(0) == 0) & (pl.program_id(1) == 0))
    def _():
        vb_ref[:, LANES:2 * LANES] = jnp.ones((LANES + seq, LANES), BF16)

    vb_ref[0:LANES, 0:LANES] = _pad_meta(vm_ref)
    vb_ref[LANES:LANES + seq, 0:LANES] = v_ref[...]
    q1, q2 = _split_q(q_ref[...])
    q1_ref[...] = q1
    q2_ref[...] = q2
    kmp = _pad_meta(km_ref)
    ri = lax.broadcasted_iota(jnp.int32, (tq, tq), 0) // CHUNK
    ci = lax.broadcasted_iota(jnp.int32, (tq, tq), 1) // CHUNK
    diag_ok = ci <= ri

    for i in range(seq // tq):
        lo = i * tq
        outs = []
        for qc_ref, s_ref in ((q1_ref, s1_ref), (q2_ref, s2_ref)):
            qc = qc_ref[lo:lo + tq, :]
            sm = _meta_scores(qc, kmp)
            s_ref[:, 0:LANES] = sm
            mrun = sm
            for j in range(i + 1):
                s = _nt_dot(qc, k_ref[j * tq:(j + 1) * tq, :])
                if j == i:
                    s = jnp.where(diag_ok, s, NEG)
                s_ref[:, LANES + j * tq:LANES + (j + 1) * tq] = s
                for c0 in range(0, tq, LANES):
                    mrun = jnp.maximum(mrun, s[:, c0:c0 + LANES])
            m = jnp.max(mrun, axis=-1, keepdims=True)
            nk = LANES + (i + 1) * tq
            acc = _dot(jnp.exp2(s_ref[:, 0:nk] - m).astype(BF16), vb_ref[0:nk, :])
            outs.append(acc[:, 0:LANES] / acc[:, LANES:2 * LANES])
        o_ref[lo:lo + tq, :] = _head_out(outs[0], outs[1], lam, gn_ref, lam0).astype(o_ref.dtype)


def _attn_frames(lam_p, q, kb, vb, kb_small, vb_small, gn, lam0, tq):
    nb = q.shape[0] // SEQ
    lam_spec = pl.BlockSpec((1, HEAD_DIM), lambda b, h: (0, 0))
    blk = pl.BlockSpec((SEQ, LANES), lambda b, h: (b, h))
    meta = pl.BlockSpec((SLOT, LANES), lambda b, h: (DEC_BATCH, h))
    return pl.pallas_call(
        functools.partial(_attn_frames_kernel, lam0=lam0, tq=tq, seq=SEQ),
        grid=(nb, N_HEADS),
        in_specs=[lam_spec] * 4 + [blk, blk, blk, meta, meta, pl.BlockSpec((1, LANES), lambda b, h: (0, 0))],
        out_specs=blk,
        out_shape=jax.ShapeDtypeStruct(q.shape, BF16),
        scratch_shapes=[pltpu.VMEM((LANES + SEQ, 2 * LANES), BF16),
                        pltpu.VMEM((SEQ, LANES), BF16), pltpu.VMEM((SEQ, LANES), BF16),
                        pltpu.VMEM((tq, LANES + SEQ), F32), pltpu.VMEM((tq, LANES + SEQ), F32)],
        compiler_params=_cparams(("arbitrary", "arbitrary")),
        name="attn_frames",
    )(*lam_p, q, kb, vb, kb_small, vb_small, gn)


def _attn_small_kernel(lq1_ref, lk1_ref, lq2_ref, lk2_ref, q_ref, k_ref, v_ref, km_ref, vm_ref,
                       ck_ref, cv_ref, gn_ref, o_ref, *, lam0):
    s = pl.program_id(0)
    lam = _lam(lq1_ref, lk1_ref, lq2_ref, lk2_ref, lam0)

    def head(h, with_cache):
        cols = slice(h * LANES, (h + 1) * LANES)
        q1, q2 = _split_q(q_ref[:, cols])
        qq = jnp.concatenate([q1, q2], axis=0)
        kmp = jnp.concatenate([km_ref[0:N_META, cols], jnp.zeros((LANES - N_META, LANES), BF16)], axis=0)
        vmp = jnp.concatenate([vm_ref[0:N_META, cols], jnp.zeros((LANES - N_META, LANES), BF16)], axis=0)
        pieces = [(_meta_scores(qq, kmp), vmp)]
        if with_cache:
            rows = pl.ds(h, PAST_LEN, stride=N_HEADS)
            pieces.append((_nt_dot(qq, ck_ref[rows, :].astype(BF16)), cv_ref[rows, :].astype(BF16)))
            pieces.append((_nt_dot(qq, k_ref[:, cols]), v_ref[:, cols]))
        o = _softmax_pv(pieces)
        o_ref[:, cols] = _head_out(o[0:SLOT], o[SLOT:2 * SLOT], lam, gn_ref, lam0).astype(o_ref.dtype)

    @pl.when(s < DEC_BATCH)
    def _():
        for h in range(N_HEADS):
            head(h, True)

    @pl.when(s == DEC_BATCH)
    def _():
        for h in range(N_HEADS):
            head(h, False)


def _attn_small(lam_p, q, k, v, cache_k, cache_v, layer, gn, lam0):
    slots = q.shape[0] // SLOT
    lam_spec = pl.BlockSpec((1, HEAD_DIM), lambda s: (0, 0))
    blk = pl.BlockSpec((SLOT, ATT_W), lambda s: (s, 0))
    meta = pl.BlockSpec((SLOT, ATT_W), lambda s: (DEC_BATCH, 0))
    cache = pl.BlockSpec((None, None, PAST_LEN * N_HEADS, V_DIM),
                         lambda s: (layer, jnp.minimum(s, DEC_BATCH - 1), 0, 0))
    return pl.pallas_call(
        functools.partial(_attn_small_kernel, lam0=lam0),
        grid=(slots,),
        in_specs=[lam_spec] * 4 + [blk, blk, blk, meta, meta, cache, cache,
                                   pl.BlockSpec((1, V_DIM), lambda s: (0, 0))],
        out_specs=blk,
        out_shape=jax.ShapeDtypeStruct(q.shape, BF16),
        compiler_params=_cparams(("arbitrary",)),
        name="attn_small",
    )(*lam_p, q, k, v, k, v, cache_k, cache_v, gn)


def _conv_kernel(hist_ref, g_ref, w_ref, b_ref, lg_ref, lb_ref, c_ref, gbuf_ref, rot_ref, y_ref, *, tt):
    gbuf_ref[0:HIST, :] = hist_ref[...]
    gbuf_ref[HIST:HIST + tt, :] = g_ref[...]
    _conv_cols(gbuf_ref, rot_ref, y_ref, w_ref, b_ref, tt)
    _ln_silu(y_ref, lg_ref, lb_ref, c_ref)


def _conv_tail(hist, g, w, b, lg, lb):
    nb, tt, _ = g.shape
    vec = pl.BlockSpec((1, CONV_W), lambda bi: (0, 0))
    return pl.pallas_call(
        functools.partial(_conv_kernel, tt=tt),
        grid=(nb,),
        in_specs=[
            pl.BlockSpec((None, HIST, CONV_W), lambda bi: (bi, 0, 0)),
            pl.BlockSpec((None, tt, CONV_W), lambda bi: (bi, 0, 0)),
            pl.BlockSpec((CONV_K, CONV_W), lambda bi: (0, 0)),
            vec, vec, vec,
        ],
        out_specs=pl.BlockSpec((None, tt, CONV_W), lambda bi: (bi, 0, 0)),
        out_shape=jax.ShapeDtypeStruct(g.shape, BF16),
        scratch_shapes=[pltpu.VMEM((HIST + tt, CONV_W), F32), pltpu.VMEM((7, HIST + tt, LANES), F32),
                        pltpu.VMEM((tt, CONV_W), F32)],
        compiler_params=_cparams(("arbitrary",)),
        name="conv_tail",
    )(hist, g, w, b, lg, lb)


def _out_mlp_kernel(att_ref, c_ref, x_ref, wa_ref, wc_ref, n2_ref, wu_hbm, wd_hbm, o_ref,
                    h2_ref, wu_buf, wd_buf, sem, *, layer, tf, nf, nsteps):
    i = pl.program_id(0)

    def tile_copies(f, slot):
        off = pl.multiple_of(f * tf, tf)
        return (pltpu.make_async_copy(wu_hbm.at[layer, :, pl.ds(off, tf)], wu_buf.at[slot], sem.at[0, slot]),
                pltpu.make_async_copy(wd_hbm.at[layer, pl.ds(off, tf), :], wd_buf.at[slot], sem.at[1, slot]))

    def start(f, slot):
        for cp in tile_copies(f, slot):
            cp.start()

    @pl.when(i == 0)
    def _():
        start(0, 0)

    x1 = x_ref[...] + _dot(att_ref[...], wa_ref[...]) + _dot(c_ref[...], wc_ref[...])
    o_ref[...] = x1
    ms = jnp.mean(x1 * x1, axis=-1, keepdims=True)
    h2_ref[...] = (x1 * lax.rsqrt(ms + EPS) * n2_ref[...]).astype(BF16)

    def body(f, carry):
        slot = f % 2
        for cp in tile_copies(f, slot):
            cp.wait()

        @pl.when(f + 1 < nf)
        def _():
            start(f + 1, 1 - slot)

        @pl.when((f + 1 == nf) & (i + 1 < nsteps))
        def _():
            start(0, 0)

        a = jnp.maximum(_dot(h2_ref[...], wu_buf[slot]), 0.0)
        o_ref[...] += _dot((a * a).astype(BF16), wd_buf[slot])
        return carry

    lax.fori_loop(0, nf, body, 0)


def _out_mlp(att, c, x, w_out, n2, w_up, w_down, layer, tm, tf):
    m = x.shape[0]
    nf = D_FF // tf
    assert nf % 2 == 0
    row = lambda i: (i, 0)
    return pl.pallas_call(
        functools.partial(_out_mlp_kernel, layer=layer, tf=tf, nf=nf, nsteps=m // tm),
        grid=(m // tm,),
        in_specs=[
            pl.BlockSpec((tm, ATT_W), row),
            pl.BlockSpec((tm, CONV_W), row),
            pl.BlockSpec((tm, D_MODEL), row),
            pl.BlockSpec((None, ATT_W, D_MODEL), lambda i: (layer, 0, 0), pipeline_mode=pl.Buffered(1)),
            pl.BlockSpec((None, CONV_W, D_MODEL), lambda i: (layer, 1, 0), pipeline_mode=pl.Buffered(1)),
            pl.BlockSpec((1, D_MODEL), lambda i: (0, 0)),
            pl.BlockSpec(memory_space=pl.ANY),
            pl.BlockSpec(memory_space=pl.ANY),
        ],
        out_specs=pl.BlockSpec((tm, D_MODEL), row),
        out_shape=jax.ShapeDtypeStruct((m, D_MODEL), F32),
        scratch_shapes=[pltpu.VMEM((tm, D_MODEL), BF16), pltpu.VMEM((2, D_MODEL, tf), BF16),
                        pltpu.VMEM((2, tf, D_MODEL), BF16), pltpu.SemaphoreType.DMA((2, 2))],
        compiler_params=_cparams(("arbitrary",)),
        name="out_mlp",
    )(att, c, x, w_out, w_out, n2, w_up, w_down)


def _rope_tables(pos):
    half = ROPE_DIM // 2
    inv_freq = ROPE_THETA ** (-(jnp.arange(half, dtype=F32) * 2.0) / ROPE_DIM)
    ang = pos.astype(F32)[:, None] * inv_freq[None, :]
    cos, sin = jnp.cos(ang), jnp.sin(ang)
    t = pos.shape[0]
    z8 = jnp.zeros((t, half), F32)
    rest0 = jnp.zeros((t, HEAD_DIM - ROPE_DIM), F32)
    c = jnp.concatenate([cos, cos, 1.0 + rest0], axis=-1)
    sa = jnp.concatenate([-sin, z8, rest0], axis=-1)
    sb = jnp.concatenate([z8, sin, rest0], axis=-1)
    return tuple(jnp.tile(a, (1, LANES // HEAD_DIM)) for a in (c, sa, sb))


def _lambda_init(layer):
    return 0.8 - 0.6 * math.exp(-0.3 * layer)


def _tile_div(n, want):
    t = want
    while n % t:
        t //= 2
    return t


def kernel(x_prompt, x_sample, cache_k, cache_v, state_conv, meta_tokens, norm1_g, w_in, q_norm_g, k_norm_g,
           lam_q1, lam_k1, lam_q2, lam_k2, attn_norm_g, conv_w, conv_b, conv_ln_g, conv_ln_b, w_out, norm2_g,
           w_up, w_down):
    assert DEC_SEQ == SLOT and N_META <= SLOT and SEQ % CHUNK == 0 and N_META <= HIST
    slots = DEC_BATCH + 1
    ms = slots * SLOT
    mf = BATCH * SEQ
    nd = DEC_BATCH * SLOT

    xf = x_prompt.reshape(mf, D_MODEL)
    meta_slot = jnp.concatenate([meta_tokens.astype(F32), jnp.zeros((SLOT - N_META, D_MODEL), F32)], axis=0)
    xs = jnp.concatenate([x_sample.reshape(nd, D_MODEL), meta_slot], axis=0)

    pos_f = N_META + jnp.arange(SEQ, dtype=jnp.int32)
    pos_s = jnp.concatenate([
        jnp.tile(N_META + PAST_LEN + jnp.arange(DEC_SEQ, dtype=jnp.int32), DEC_BATCH),
        jnp.arange(N_META, dtype=jnp.int32), jnp.zeros((SLOT - N_META,), jnp.int32)])
    tab_f = _rope_tables(pos_f)
    tab_s = _rope_tables(pos_s)

    grp = jnp.arange(MXU_W) // HEAD_DIM
    p256 = jnp.where(grp[:, None] == grp[None, :], 1.0 / HEAD_DIM, 0.0).astype(BF16)

    w_in_b = w_in.astype(BF16)
    w_out_b = w_out.astype(BF16)
    w_up_b = w_up.astype(BF16)
    w_down_b = w_down.astype(BF16)
    ck = cache_k.reshape(DEPTH, DEC_BATCH, PAST_LEN * N_HEADS, V_DIM)
    cv = cache_v.reshape(DEPTH, DEC_BATCH, PAST_LEN * N_HEADS, V_DIM)

    tm_f = _tile_div(SEQ, 512)
    tm_s = ms // 2
    tq = _tile_div(SEQ, 256)

    k_p, v_p, c_p, k_s, v_s, c_s = [], [], [], [], [], []
    for l in range(DEPTH):
        lam0 = _lambda_init(l)
        lam_p = tuple(a[l].reshape(1, HEAD_DIM) for a in (lam_q1, lam_k1, lam_q2, lam_k2))
        g1 = norm1_g[l].reshape(1, D_MODEL)
        gq = jnp.tile(q_norm_g[l] * (ATTN_SCALE * math.log2(math.e)), LANES // HEAD_DIM).reshape(1, LANES)
        gk = jnp.tile(k_norm_g[l], LANES // HEAD_DIM).reshape(1, LANES)
        gn = attn_norm_g[l].reshape(1, V_DIM)
        cw, cb = conv_w[l], conv_b[l].reshape(1, CONV_W)
        lg, lb = conv_ln_g[l].reshape(1, CONV_W), conv_ln_b[l].reshape(1, CONV_W)
        n2 = norm2_g[l].reshape(1, D_MODEL)

        qs, ks, vs, kbs, vbs, gs = _in_proj(xs, g1, w_in_b, l, p256, gq, gk, *tab_s, tm_s)
        g_meta = gs[nd:nd + N_META]
        hist_f = jnp.concatenate([jnp.zeros((HIST - N_META, CONV_W), F32), g_meta], axis=0)
        qf, kf, vf, kbf, vbf, c_f, g_tail = _in_proj(xf, g1, w_in_b, l, p256, gq, gk, *tab_f, tm_f,
                                                     conv=(hist_f, cw, cb, lg, lb), seq=SEQ)

        att_f = _attn_frames(lam_p, qf, kbf, vbf, kbs, vbs, gn, lam0, tq)
        att_s = _attn_small(lam_p, qs, kbs, vbs, ck, cv, l, gn, lam0)

        hist_s = jnp.concatenate([
            jnp.pad(state_conv[l].astype(F32), ((0, 0), (HIST - (CONV_K - 1), 0), (0, 0))),
            jnp.zeros((1, HIST, CONV_W), F32)], axis=0)
        c_sm = _conv_tail(hist_s, gs.reshape(slots, SLOT, CONV_W), cw, cb, lg, lb).reshape(ms, CONV_W)

        xf = _out_mlp(att_f, c_f, xf, w_out_b, n2, w_up_b, w_down_b, l, tm_f, 1024)
        xs = _out_mlp(att_s, c_sm, xs, w_out_b, n2, w_up_b, w_down_b, l, tm_s, 1024)

        k_meta = jnp.broadcast_to(ks[nd:nd + N_META][None], (BATCH, N_META, ATT_W))
        v_meta = jnp.broadcast_to(vs[nd:nd + N_META][None], (BATCH, N_META, ATT_W))
        k_p.append(jnp.concatenate([k_meta, kf.reshape(BATCH, SEQ, ATT_W)], axis=1))
        v_p.append(jnp.concatenate([v_meta, vf.reshape(BATCH, SEQ, ATT_W)], axis=1))
        g_all = jnp.concatenate([jnp.broadcast_to(hist_f[None], (BATCH, HIST, CONV_W)), g_tail], axis=1)
        c_p.append(g_all[:, -(CONV_K - 1):])
        k_s.append(ks[:nd].reshape(DEC_BATCH, DEC_SEQ, ATT_W))
        v_s.append(vs[:nd].reshape(DEC_BATCH, DEC_SEQ, ATT_W))
        gs_all = jnp.concatenate([hist_s[:DEC_BATCH], gs[:nd].reshape(DEC_BATCH, DEC_SEQ, CONV_W)], axis=1)
        c_s.append(gs_all[:, -(CONV_K - 1):])

    lp = N_META + SEQ
    y_prompt = xf.reshape(BATCH, SEQ, D_MODEL)
    y_sample = xs[:nd].reshape(DEC_BATCH, DEC_SEQ, D_MODEL)
    return (y_prompt, y_sample,
            jnp.stack(k_p).reshape(DEPTH, BATCH, lp, N_HEADS, V_DIM),
            jnp.stack(v_p).reshape(DEPTH, BATCH, lp, N_HEADS, V_DIM),
            jnp.stack(c_p),
            jnp.stack(k_s).reshape(DEPTH, DEC_BATCH, DEC_SEQ, N_HEADS, V_DIM),
            jnp.stack(v_s).reshape(DEPTH, DEC_BATCH, DEC_SEQ, N_HEADS, V_DIM),
            jnp.stack(c_s))
```

```python
import functools
import math

import jax
import jax.numpy as jnp
from jax import lax
from jax.experimental import pallas as pl
from jax.experimental.pallas import tpu as pltpu

F32 = jnp.float32
BF16 = jnp.bfloat16

D_MODEL = 2048
BATCH = 16
SEQ = 2048
DEPTH = 4
DEC_BATCH = 16
DEC_SEQ = 64
PAST_LEN = 2048
CHUNK = 64
N_META = 16
N_HEADS = 8
HEAD_DIM = 64
V_DIM = 2 * HEAD_DIM
ATT_W = N_HEADS * V_DIM
CONV_W = D_MODEL - ATT_W
IN_W = 3 * ATT_W + 2 * CONV_W
CONV_K = 31
ROPE_DIM = HEAD_DIM // 4
ROPE_THETA = 500000.0
D_FF = 4 * D_MODEL
EPS = 1e-6
NEG = -1e30
ATTN_SCALE = HEAD_DIM ** -0.5

SLOT = DEC_SEQ
HIST = 32
LANES = 128
MXU_W = 256
VMEM_LIMIT = 60 * 1024 * 1024


def _cparams(sem):
    return pltpu.CompilerParams(dimension_semantics=sem, vmem_limit_bytes=VMEM_LIMIT)


def _nt_dot(a, b):
    return lax.dot_general(a, b, (((1,), (1,)), ((), ())), preferred_element_type=F32)


def _dot(a, b):
    return jnp.dot(a, b, preferred_element_type=F32)


CONV_ROWS = 32
CONV_UNROLL = 4


def _conv_cols(gbuf_ref, rot_ref, y_ref, cw_ref, cb_ref, tm):
    off = HIST - (CONV_K - 1)
    win = HIST + tm

    def col_body(ci, carry):
        c0 = pl.multiple_of(ci * LANES, LANES)
        xcol = gbuf_ref[:, pl.ds(c0, LANES)]
        for res in range(1, 8):
            rot_ref[res - 1] = pltpu.roll(xcol, win - res, 0)

        def row_body(rb, carry2):
            r0 = pl.multiple_of(rb * CONV_ROWS, CONV_ROWS)
            acc = jnp.zeros((CONV_ROWS, LANES), F32)
            for s in range(off, off + CONV_K):
                res = s % 8
                rows = pl.ds(r0 + (s - res), CONV_ROWS)
                src = gbuf_ref[rows, pl.ds(c0, LANES)] if res == 0 else rot_ref[res - 1, rows, :]
                acc = acc + src * cw_ref[s - off:s - off + 1, pl.ds(c0, LANES)]
            y_ref[pl.ds(r0, CONV_ROWS), pl.ds(c0, LANES)] = acc + cb_ref[:, pl.ds(c0, LANES)]
            return carry2

        lax.fori_loop(0, tm // CONV_ROWS, row_body, 0, unroll=min(CONV_UNROLL, tm // CONV_ROWS))
        return carry

    lax.fori_loop(0, CONV_W // LANES, col_body, 0)


def _ln_silu(y_ref, lg_ref, lb_ref, c_ref):
    y = y_ref[...]
    yc = y - jnp.mean(y, axis=-1, keepdims=True)
    yn = yc * lax.rsqrt(jnp.mean(yc * yc, axis=-1, keepdims=True) + EPS)
    yn = yn * lg_ref[...] + lb_ref[...]
    c_ref[...] = (yn * jax.nn.sigmoid(yn)).astype(c_ref.dtype)


def _in_proj_kernel(x_ref, g1_ref, w_ref, p_ref, gq_ref, gk_ref, c_ref, sa_ref, sb_ref, *rest, tm, fuse_conv, tiles_per_seq):
    if fuse_conv:
        (hist_ref, cw_ref, cb_ref, lg_ref, lb_ref,
         q_ref, k_ref, v_ref, kb_ref, vb_ref, co_ref, gt_ref, gbuf_ref, rot_ref, y_ref) = rest
    else:
        q_ref, k_ref, v_ref, kb_ref, vb_ref, g_ref = rest
    x = x_ref[...]
    ms = jnp.mean(x * x, axis=-1, keepdims=True)
    xn = (x * lax.rsqrt(ms + EPS) * g1_ref[...]).astype(BF16)

    def qk_norm_rope(z, gain_ref, out_refs):
        for s in range(ATT_W // MXU_W):
            zz = z[:, s * MXU_W:(s + 1) * MXU_W]
            ms = _dot((zz * zz).astype(BF16), p_ref[...])
            y = zz * lax.rsqrt(ms + EPS)
            for hh in range(MXU_W // LANES):
                lo = s * MXU_W + hh * LANES
                yh = y[:, hh * LANES:(hh + 1) * LANES] * gain_ref[...]
                r = (yh * c_ref[...]
                     + pltpu.roll(yh, LANES - ROPE_DIM // 2, 1) * sa_ref[...]
                     + pltpu.roll(yh, ROPE_DIM // 2, 1) * sb_ref[...])
                for o in out_refs:
                    o[:, lo:lo + LANES] = r.astype(o.dtype)

    a = _dot(xn, w_ref[:, 3 * ATT_W:3 * ATT_W + CONV_W])
    gate = _dot(xn, w_ref[:, 3 * ATT_W + CONV_W:3 * ATT_W + 2 * CONV_W])
    g = a * jax.nn.sigmoid(gate)
    if fuse_conv:
        first = pl.program_id(0) % tiles_per_seq == 0

        @pl.when(first)
        def _():
            gbuf_ref[0:HIST, :] = hist_ref[...]

        @pl.when(jnp.logical_not(first))
        def _():
            gbuf_ref[0:HIST, :] = gbuf_ref[tm:tm + HIST, :]

        gbuf_ref[HIST:HIST + tm, :] = g
        gt_ref[...] = g[tm - HIST:tm, :]
    else:
        g_ref[...] = g
    qk_norm_rope(_dot(xn, w_ref[:, 0:ATT_W]), gq_ref, (q_ref,))
    qk_norm_rope(_dot(xn, w_ref[:, ATT_W:2 * ATT_W]), gk_ref, (k_ref, kb_ref))
    v = _dot(xn, w_ref[:, 2 * ATT_W:3 * ATT_W])
    v_ref[...] = v
    vb_ref[...] = v.astype(BF16)
    if fuse_conv:
        _conv_cols(gbuf_ref, rot_ref, y_ref, cw_ref, cb_ref, tm)
        _ln_silu(y_ref, lg_ref, lb_ref, co_ref)


def _in_proj(x, g1, w_in, layer, p256, gq, gk, c_t, sa_t, sb_t, tm, conv=None, seq=None):
    m = x.shape[0]
    nt = c_t.shape[0] // tm
    row = lambda i: (i, 0)
    const = lambda i: (0, 0)
    tab = lambda i: (i % nt, 0)
    f32_out = jax.ShapeDtypeStruct((m, ATT_W), F32)
    bf_out = jax.ShapeDtypeStruct((m, ATT_W), BF16)
    in_specs = [
        pl.BlockSpec((tm, D_MODEL), row),
        pl.BlockSpec((1, D_MODEL), const),
        pl.BlockSpec((None, D_MODEL, IN_W), lambda i: (layer, 0, 0), pipeline_mode=pl.Buffered(1)),
        pl.BlockSpec((MXU_W, MXU_W), const),
        pl.BlockSpec((1, LANES), const),
        pl.BlockSpec((1, LANES), const),
        pl.BlockSpec((tm, LANES), tab),
        pl.BlockSpec((tm, LANES), tab),
        pl.BlockSpec((tm, LANES), tab),
    ]
    out_specs = [pl.BlockSpec((tm, ATT_W), row)] * 5 + [pl.BlockSpec((tm, CONV_W), row)]
    args = [x, g1, w_in, p256, gq, gk, c_t, sa_t, sb_t]
    if conv is None:
        out_shape = [bf_out, f32_out, f32_out, bf_out, bf_out, jax.ShapeDtypeStruct((m, CONV_W), F32)]
        scratch = []
        tps = 1
    else:
        tps = seq // tm
        vec = pl.BlockSpec((1, CONV_W), const)
        in_specs += [pl.BlockSpec((HIST, CONV_W), const), pl.BlockSpec((CONV_K, CONV_W), const), vec, vec, vec]
        args += list(conv)
        out_specs = out_specs + [pl.BlockSpec((None, HIST, CONV_W), lambda i: (i // tps, 0, 0))]
        out_shape = [bf_out, f32_out, f32_out, bf_out, bf_out, jax.ShapeDtypeStruct((m, CONV_W), BF16),
                     jax.ShapeDtypeStruct((m // seq, HIST, CONV_W), F32)]
        scratch = [pltpu.VMEM((HIST + tm, CONV_W), F32), pltpu.VMEM((7, HIST + tm, LANES), F32),
                   pltpu.VMEM((tm, CONV_W), F32)]
    return pl.pallas_call(
        functools.partial(_in_proj_kernel, tm=tm, fuse_conv=conv is not None, tiles_per_seq=tps),
        grid=(m // tm,),
        in_specs=in_specs,
        out_specs=out_specs,
        out_shape=out_shape,
        scratch_shapes=scratch,
        compiler_params=_cparams(("arbitrary",)),
        name="in_proj",
    )(*args)


def _lam(lq1_ref, lk1_ref, lq2_ref, lk2_ref, lam0):
    a = jnp.sum(lq1_ref[...] * lk1_ref[...], axis=-1, keepdims=True)
    b = jnp.sum(lq2_ref[...] * lk2_ref[...], axis=-1, keepdims=True)
    return jnp.exp(a) - jnp.exp(b) + lam0


def _split_q(q):
    lane = lax.broadcasted_iota(jnp.int32, q.shape, 1)
    zero = jnp.zeros_like(q)
    return jnp.where(lane < HEAD_DIM, q, zero), jnp.where(lane >= HEAD_DIM, q, zero)


def _pad_meta(ref):
    return jnp.concatenate([ref[0:N_META, :], jnp.zeros((LANES - N_META, LANES), BF16)], axis=0)


def _meta_scores(qc, kmp):
    s = _nt_dot(qc, kmp)
    lane = lax.broadcasted_iota(jnp.int32, s.shape, 1)
    return jnp.where(lane < N_META, s, NEG)


def _softmax_pv(pieces):
    m = None
    for s, _ in pieces:
        r = jnp.max(s, axis=-1, keepdims=True)
        m = r if m is None else jnp.maximum(m, r)
    l = None
    o = None
    for s, v in pieces:
        e = jnp.exp2(s - m)
        ls = jnp.sum(e, axis=-1, keepdims=True)
        os_ = _dot(e.astype(BF16), v)
        l = ls if l is None else l + ls
        o = os_ if o is None else o + os_
    return o / l


def _head_out(o1, o2, lam, gn_ref, lam0):
    o = o1 - lam * o2
    y = o * lax.rsqrt(jnp.mean(o * o, axis=-1, keepdims=True) + EPS) * gn_ref[...]
    return y * (1.0 - lam0)


def _attn_frames_kernel(lq1_ref, lk1_ref, lq2_ref, lk2_ref, q_ref, k_ref, v_ref, km_ref, vm_ref,
                        gn_ref, o_ref, vb_ref, q1_ref, q2_ref, s1_ref, s2_ref, *, lam0, tq, seq):
    lam = _lam(lq1_ref, lk1_ref, lq2_ref, lk2_ref, lam0)

    @pl.when(pl.program_id(0) == 0)
    def _():
        vb_ref[:, LANES:2 * LANES] = jnp.ones((LANES + seq, LANES), BF16)

    ri = lax.broadcasted_iota(jnp.int32, (tq, tq), 0) // CHUNK
    ci = lax.broadcasted_iota(jnp.int32, (tq, tq), 1) // CHUNK
    diag_ok = ci <= ri
    pad = jnp.zeros((LANES - N_META, LANES), BF16)

    def head(h, carry):
        cols = pl.ds(pl.multiple_of(h * LANES, LANES), LANES)
        vb_ref[0:LANES, 0:LANES] = jnp.concatenate([vm_ref[0:N_META, cols], pad], axis=0)
        vb_ref[LANES:LANES + seq, 0:LANES] = v_ref[:, cols]
        q1, q2 = _split_q(q_ref[:, cols])
        q1_ref[...] = q1
        q2_ref[...] = q2
        kmp = jnp.concatenate([km_ref[0:N_META, cols], pad], axis=0)

        for i in range(seq // tq):
            lo = i * tq
            outs = []
            for qc_ref, s_ref in ((q1_ref, s1_ref), (q2_ref, s2_ref)):
                qc = qc_ref[lo:lo + tq, :]
                sm = _meta_scores(qc, kmp)
                s_ref[:, 0:LANES] = sm
                mrun = sm
                for j in range(i + 1):
                    s = _nt_dot(qc, k_ref[j * tq:(j + 1) * tq, cols])
                    if j == i:
                        s = jnp.where(diag_ok, s, NEG)
                    s_ref[:, LANES + j * tq:LANES + (j + 1) * tq] = s
                    for c0 in range(0, tq, LANES):
                        mrun = jnp.maximum(mrun, s[:, c0:c0 + LANES])
                m = jnp.max(mrun, axis=-1, keepdims=True)
                nk = LANES + (i + 1) * tq
                acc = _dot(jnp.exp2(s_ref[:, 0:nk] - m).astype(BF16), vb_ref[0:nk, :])
                outs.append(acc[:, 0:LANES] / acc[:, LANES:2 * LANES])
            o_ref[lo:lo + tq, cols] = _head_out(outs[0], outs[1], lam, gn_ref, lam0).astype(o_ref.dtype)
        return carry

    lax.fori_loop(0, N_HEADS, head, 0)


def _attn_frames(lam_p, q, kb, vb, kb_small, vb_small, gn, lam0, tq):
    nb = q.shape[0] // SEQ
    lam_spec = pl.BlockSpec((1, HEAD_DIM), lambda b: (0, 0))
    blk = pl.BlockSpec((SEQ, ATT_W), lambda b: (b, 0))
    meta = pl.BlockSpec((SLOT, ATT_W), lambda b: (DEC_BATCH, 0))
    return pl.pallas_call(
        functools.partial(_attn_frames_kernel, lam0=lam0, tq=tq, seq=SEQ),
        grid=(nb,),
        in_specs=[lam_spec] * 4 + [blk, blk, blk, meta, meta, pl.BlockSpec((1, LANES), lambda b: (0, 0))],
        out_specs=blk,
        out_shape=jax.ShapeDtypeStruct(q.shape, BF16),
        scratch_shapes=[pltpu.VMEM((LANES + SEQ, 2 * LANES), BF16),
                        pltpu.VMEM((SEQ, LANES), BF16), pltpu.VMEM((SEQ, LANES), BF16),
                        pltpu.VMEM((tq, LANES + SEQ), F32), pltpu.VMEM((tq, LANES + SEQ), F32)],
        compiler_params=_cparams(("arbitrary",)),
        name="attn_frames",
    )(*lam_p, q, kb, vb, kb_small, vb_small, gn)


def _attn_small_kernel(lq1_ref, lk1_ref, lq2_ref, lk2_ref, q_ref, k_ref, v_ref, km_ref, vm_ref,
                       ck_ref, cv_ref, gn_ref, o_ref, *, lam0):
    s = pl.program_id(0)
    lam = _lam(lq1_ref, lk1_ref, lq2_ref, lk2_ref, lam0)

    def head(h, with_cache):
        cols = slice(h * LANES, (h + 1) * LANES)
        q1, q2 = _split_q(q_ref[:, cols])
        qq = jnp.concatenate([q1, q2], axis=0)
        kmp = jnp.concatenate([km_ref[0:N_META, cols], jnp.zeros((LANES - N_META, LANES), BF16)], axis=0)
        vmp = jnp.concatenate([vm_ref[0:N_META, cols], jnp.zeros((LANES - N_META, LANES), BF16)], axis=0)
        pieces = [(_meta_scores(qq, kmp), vmp)]
        if with_cache:
            rows = pl.ds(h, PAST_LEN, stride=N_HEADS)
            pieces.append((_nt_dot(qq, ck_ref[rows, :].astype(BF16)), cv_ref[rows, :].astype(BF16)))
            pieces.append((_nt_dot(qq, k_ref[:, cols]), v_ref[:, cols]))
        o = _softmax_pv(pieces)
        o_ref[:, cols] = _head_out(o[0:SLOT], o[SLOT:2 * SLOT], lam, gn_ref, lam0).astype(o_ref.dtype)

    @pl.when(s < DEC_BATCH)
    def _():
        for h in range(N_HEADS):
            head(h, True)

    @pl.when(s == DEC_BATCH)
    def _():
        for h in range(N_HEADS):
            head(h, False)


def _attn_small(lam_p, q, k, v, cache_k, cache_v, layer, gn, lam0):
    slots = q.shape[0] // SLOT
    lam_spec = pl.BlockSpec((1, HEAD_DIM), lambda s: (0, 0))
    blk = pl.BlockSpec((SLOT, ATT_W), lambda s: (s, 0))
    meta = pl.BlockSpec((SLOT, ATT_W), lambda s: (DEC_BATCH, 0))
    cache = pl.BlockSpec((None, None, PAST_LEN * N_HEADS, V_DIM),
                         lambda s: (layer, jnp.minimum(s, DEC_BATCH - 1), 0, 0))
    return pl.pallas_call(
        functools.partial(_attn_small_kernel, lam0=lam0),
        grid=(slots,),
        in_specs=[lam_spec] * 4 + [blk, blk, blk, meta, meta, cache, cache,
                                   pl.BlockSpec((1, V_DIM), lambda s: (0, 0))],
        out_specs=blk,
        out_shape=jax.ShapeDtypeStruct(q.shape, BF16),
        compiler_params=_cparams(("arbitrary",)),
        name="attn_small",
    )(*lam_p, q, k, v, k, v, cache_k, cache_v, gn)


def _conv_kernel(hist_ref, g_ref, w_ref, b_ref, lg_ref, lb_ref, c_ref, gbuf_ref, rot_ref, y_ref, *, tt):
    gbuf_ref[0:HIST, :] = hist_ref[...]
    gbuf_ref[HIST:HIST + tt, :] = g_ref[...]
    _conv_cols(gbuf_ref, rot_ref, y_ref, w_ref, b_ref, tt)
    _ln_silu(y_ref, lg_ref, lb_ref, c_ref)


def _conv_tail(hist, g, w, b, lg, lb):
    nb, tt, _ = g.shape
    vec = pl.BlockSpec((1, CONV_W), lambda bi: (0, 0))
    return pl.pallas_call(
        functools.partial(_conv_kernel, tt=tt),
        grid=(nb,),
        in_specs=[
            pl.BlockSpec((None, HIST, CONV_W), lambda bi: (bi, 0, 0)),
            pl.BlockSpec((None, tt, CONV_W), lambda bi: (bi, 0, 0)),
            pl.BlockSpec((CONV_K, CONV_W), lambda bi: (0, 0)),
            vec, vec, vec,
        ],
        out_specs=pl.BlockSpec((None, tt, CONV_W), lambda bi: (bi, 0, 0)),
        out_shape=jax.ShapeDtypeStruct(g.shape, BF16),
        scratch_shapes=[pltpu.VMEM((HIST + tt, CONV_W), F32), pltpu.VMEM((7, HIST + tt, LANES), F32),
                        pltpu.VMEM((tt, CONV_W), F32)],
        compiler_params=_cparams(("arbitrary",)),
        name="conv_tail",
    )(hist, g, w, b, lg, lb)


def _out_mlp_kernel(att_ref, c_ref, x_ref, wa_ref, wc_ref, n2_ref, wu_hbm, wd_hbm, o_ref,
                    h2_ref, wu_buf, wd_buf, sem, *, layer, tf, nf, nsteps):
    i = pl.program_id(0)

    def tile_copies(f, slot):
        off = pl.multiple_of(f * tf, tf)
        return (pltpu.make_async_copy(wu_hbm.at[layer, :, pl.ds(off, tf)], wu_buf.at[slot], sem.at[0, slot]),
                pltpu.make_async_copy(wd_hbm.at[layer, pl.ds(off, tf), :], wd_buf.at[slot], sem.at[1, slot]))

    def start(f, slot):
        for cp in tile_copies(f, slot):
            cp.start()

    @pl.when(i == 0)
    def _():
        start(0, 0)

    x1 = x_ref[...] + _dot(att_ref[...], wa_ref[...]) + _dot(c_ref[...], wc_ref[...])
    o_ref[...] = x1
    ms = jnp.mean(x1 * x1, axis=-1, keepdims=True)
    h2_ref[...] = (x1 * lax.rsqrt(ms + EPS) * n2_ref[...]).astype(BF16)

    def body(f, carry):
        slot = f % 2
        for cp in tile_copies(f, slot):
            cp.wait()

        @pl.when(f + 1 < nf)
        def _():
            start(f + 1, 1 - slot)

        @pl.when((f + 1 == nf) & (i + 1 < nsteps))
        def _():
            start(0, 0)

        a = jnp.maximum(_dot(h2_ref[...], wu_buf[slot]), 0.0)
        o_ref[...] += _dot((a * a).astype(BF16), wd_buf[slot])
        return carry

    lax.fori_loop(0, nf, body, 0)


def _out_mlp(att, c, x, w_out, n2, w_up, w_down, layer, tm, tf):
    m = x.shape[0]
    nf = D_FF // tf
    assert nf % 2 == 0
    row = lambda i: (i, 0)
    return pl.pallas_call(
        functools.partial(_out_mlp_kernel, layer=layer, tf=tf, nf=nf, nsteps=m // tm),
        grid=(m // tm,),
        in_specs=[
            pl.BlockSpec((tm, ATT_W), row),
            pl.BlockSpec((tm, CONV_W), row),
            pl.BlockSpec((tm, D_MODEL), row),
            pl.BlockSpec((None, ATT_W, D_MODEL), lambda i: (layer, 0, 0), pipeline_mode=pl.Buffered(1)),
            pl.BlockSpec((None, CONV_W, D_MODEL), lambda i: (layer, 1, 0), pipeline_mode=pl.Buffered(1)),
            pl.BlockSpec((1, D_MODEL), lambda i: (0, 0)),
            pl.BlockSpec(memory_space=pl.ANY),
            pl.BlockSpec(memory_space=pl.ANY),
        ],
        out_specs=pl.BlockSpec((tm, D_MODEL), row),
        out_shape=jax.ShapeDtypeStruct((m, D_MODEL), F32),
        scratch_shapes=[pltpu.VMEM((tm, D_MODEL), BF16), pltpu.VMEM((2, D_MODEL, tf), BF16),
                        pltpu.VMEM((2, tf, D_MODEL), BF16), pltpu.SemaphoreType.DMA((2, 2))],
        compiler_params=_cparams(("arbitrary",)),
        name="out_mlp",
    )(att, c, x, w_out, w_out, n2, w_up, w_down)


def _rope_tables(pos):
    half = ROPE_DIM // 2
    inv_freq = ROPE_THETA ** (-(jnp.arange(half, dtype=F32) * 2.0) / ROPE_DIM)
    ang = pos.astype(F32)[:, None] * inv_freq[None, :]
    cos, sin = jnp.cos(ang), jnp.sin(ang)
    t = pos.shape[0]
    z8 = jnp.zeros((t, half), F32)
    rest0 = jnp.zeros((t, HEAD_DIM - ROPE_DIM), F32)
    c = jnp.concatenate([cos, cos, 1.0 + rest0], axis=-1)
    sa = jnp.concatenate([-sin, z8, rest0], axis=-1)
    sb = jnp.concatenate([z8, sin, rest0], axis=-1)
    return tuple(jnp.tile(a, (1, LANES // HEAD_DIM)) for a in (c, sa, sb))


def _lambda_init(layer):
    return 0.8 - 0.6 * math.exp(-0.3 * layer)


def _tile_div(n, want):
    t = want
    while n % t:
        t //= 2
    return t


def kernel(x_prompt, x_sample, cache_k, cache_v, state_conv, meta_tokens, norm1_g, w_in, q_norm_g, k_norm_g,
           lam_q1, lam_k1, lam_q2, lam_k2, attn_norm_g, conv_w, conv_b, conv_ln_g, conv_ln_b, w_out, norm2_g,
           w_up, w_down):
    assert DEC_SEQ == SLOT and N_META <= SLOT and SEQ % CHUNK == 0 and N_META <= HIST
    slots = DEC_BATCH + 1
    ms = slots * SLOT
    mf = BATCH * SEQ
    nd = DEC_BATCH * SLOT

    xf = x_prompt.reshape(mf, D_MODEL)
    meta_slot = jnp.concatenate([meta_tokens.astype(F32), jnp.zeros((SLOT - N_META, D_MODEL), F32)], axis=0)
    xs = jnp.concatenate([x_sample.reshape(nd, D_MODEL), meta_slot], axis=0)

    pos_f = N_META + jnp.arange(SEQ, dtype=jnp.int32)
    pos_s = jnp.concatenate([
        jnp.tile(N_META + PAST_LEN + jnp.arange(DEC_SEQ, dtype=jnp.int32), DEC_BATCH),
        jnp.arange(N_META, dtype=jnp.int32), jnp.zeros((SLOT - N_META,), jnp.int32)])
    tab_f = _rope_tables(pos_f)
    tab_s = _rope_tables(pos_s)

    grp = jnp.arange(MXU_W) // HEAD_DIM
    p256 = jnp.where(grp[:, None] == grp[None, :], 1.0 / HEAD_DIM, 0.0).astype(BF16)

    w_in_b = w_in.astype(BF16)
    w_out_b = w_out.astype(BF16)
    w_up_b = w_up.astype(BF16)
    w_down_b = w_down.astype(BF16)
    ck = cache_k.reshape(DEPTH, DEC_BATCH, PAST_LEN * N_HEADS, V_DIM)
    cv = cache_v.reshape(DEPTH, DEC_BATCH, PAST_LEN * N_HEADS, V_DIM)

    tm_f = _tile_div(SEQ, 512)
    tm_s = ms // 2
    tq = _tile_div(SEQ, 256)

    k_p, v_p, c_p, k_s, v_s, c_s = [], [], [], [], [], []
    for l in range(DEPTH):
        lam0 = _lambda_init(l)
        lam_p = tuple(a[l].reshape(1, HEAD_DIM) for a in (lam_q1, lam_k1, lam_q2, lam_k2))
        g1 = norm1_g[l].reshape(1, D_MODEL)
        gq = jnp.tile(q_norm_g[l] * (ATTN_SCALE * math.log2(math.e)), LANES // HEAD_DIM).reshape(1, LANES)
        gk = jnp.tile(k_norm_g[l], LANES // HEAD_DIM).reshape(1, LANES)
        gn = attn_norm_g[l].reshape(1, V_DIM)
        cw, cb = conv_w[l], conv_b[l].reshape(1, CONV_W)
        lg, lb = conv_ln_g[l].reshape(1, CONV_W), conv_ln_b[l].reshape(1, CONV_W)
        n2 = norm2_g[l].reshape(1, D_MODEL)

        qs, ks, vs, kbs, vbs, gs = _in_proj(xs, g1, w_in_b, l, p256, gq, gk, *tab_s, tm_s)
        g_meta = gs[nd:nd + N_META]
        hist_f = jnp.concatenate([jnp.zeros((HIST - N_META, CONV_W), F32), g_meta], axis=0)
        qf, kf, vf, kbf, vbf, c_f, g_tail = _in_proj(xf, g1, w_in_b, l, p256, gq, gk, *tab_f, tm_f,
                                                     conv=(hist_f, cw, cb, lg, lb), seq=SEQ)

        att_f = _attn_frames(lam_p, qf, kbf, vbf, kbs, vbs, gn, lam0, tq)
        att_s = _attn_small(lam_p, qs, kbs, vbs, ck, cv, l, gn, lam0)

        hist_s = jnp.concatenate([
            jnp.pad(state_conv[l].astype(F32), ((0, 0), (HIST - (CONV_K - 1), 0), (0, 0))),
            jnp.zeros((1, HIST, CONV_W), F32)], axis=0)
        c_sm = _conv_tail(hist_s, gs.reshape(slots, SLOT, CONV_W), cw, cb, lg, lb).reshape(ms, CONV_W)

        xf = _out_mlp(att_f, c_f, xf, w_out_b, n2, w_up_b, w_down_b, l, tm_f, 1024)
        xs = _out_mlp(att_s, c_sm, xs, w_out_b, n2, w_up_b, w_down_b, l, tm_s, 1024)

        k_meta = jnp.broadcast_to(ks[nd:nd + N_META][None], (BATCH, N_META, ATT_W))
        v_meta = jnp.broadcast_to(vs[nd:nd + N_META][None], (BATCH, N_META, ATT_W))
        k_p.append(jnp.concatenate([k_meta, kf.reshape(BATCH, SEQ, ATT_W)], axis=1))
        v_p.append(jnp.concatenate([v_meta, vf.reshape(BATCH, SEQ, ATT_W)], axis=1))
        g_all = jnp.concatenate([jnp.broadcast_to(hist_f[None], (BATCH, HIST, CONV_W)), g_tail], axis=1)
        c_p.append(g_all[:, -(CONV_K - 1):])
        k_s.append(ks[:nd].reshape(DEC_BATCH, DEC_SEQ, ATT_W))
        v_s.append(vs[:nd].reshape(DEC_BATCH, DEC_SEQ, ATT_W))
        gs_all = jnp.concatenate([hist_s[:DEC_BATCH], gs[:nd].reshape(DEC_BATCH, DEC_SEQ, CONV_W)], axis=1)
        c_s.append(gs_all[:, -(CONV_K - 1):])

    lp = N_META + SEQ
    y_prompt = xf.reshape(BATCH, SEQ, D_MODEL)
    y_sample = xs[:nd].reshape(DEC_BATCH, DEC_SEQ, D_MODEL)
    return (y_prompt, y_sample,
            jnp.stack(k_p).reshape(DEPTH, BATCH, lp, N_HEADS, V_DIM),
            jnp.stack(v_p).reshape(DEPTH, BATCH, lp, N_HEADS, V_DIM),
            jnp.stack(c_p),
            jnp.stack(k_s).reshape(DEPTH, DEC_BATCH, DEC_SEQ, N_HEADS, V_DIM),
            jnp.stack(v_s).reshape(DEPTH, DEC_BATCH, DEC_SEQ, N_HEADS, V_DIM),
            jnp.stack(c_s))
```
